```python
import jax, jax.numpy as jnp
from jax import lax
import numpy as np

D_MODEL = 2048
BATCH = 8
SEQ = 8192
DEPTH = 4

CONV_CHANNELS = D_MODEL
CONV_WIDTH = 31
HEAD_DIM_K = 128
HEAD_DIM_V = 128
N_HEADS = D_MODEL // 128
KEY_DIM = N_HEADS * HEAD_DIM_K
VALUE_DIM = N_HEADS * HEAD_DIM_V
SHORT_CONV = 4
CHUNK = 64
D_FF = -((-8 * D_MODEL) // (3 * 256)) * 256
DEEPNORM_ALPHA = (2.0 * DEPTH) ** 0.25
DEEPNORM_BETA = (8.0 * DEPTH) ** -0.25
LN_EPS = 1e-5
IN_SPLITS = (CONV_CHANNELS, CONV_CHANNELS, 2 * KEY_DIM + VALUE_DIM, VALUE_DIM,
             N_HEADS, N_HEADS, D_MODEL, D_MODEL)
IN_WIDTH = sum(IN_SPLITS)

kernel_name = "hybrid_conformer_gdn_deepnorm"


def layer_norm(x, g, b):
    xf = x.astype(jnp.float32)
    mu = jnp.mean(xf, axis=-1, keepdims=True)
    var = jnp.mean(jnp.square(xf - mu), axis=-1, keepdims=True)
    y = (xf - mu) * lax.rsqrt(var + LN_EPS) * g.astype(jnp.float32) + b.astype(jnp.float32)
    return y.astype(x.dtype)


def causal_depthwise_conv(x, w):
    k, c = w.shape
    return lax.conv_general_dilated(
        x, w[:, None, :], window_strides=(1,), padding=[(k - 1, 0)],
        dimension_numbers=('NWC', 'WIO', 'NWC'), feature_group_count=c)


def l2_normalize(x):
    return x * lax.rsqrt(jnp.sum(jnp.square(x), axis=-1, keepdims=True) + 1e-6)


def gated_delta_rule(q, k, v, beta, log_a):
    bsz, seq, h, dk = q.shape
    dv = v.shape[-1]
    n = seq // CHUNK

    def chunks(t):
        t = t.reshape((bsz, n, CHUNK, h) + t.shape[3:])
        return jnp.moveaxis(t, 3, 1)

    q = chunks(q) * (dk ** -0.5)
    k, v, beta = chunks(k), chunks(v), chunks(beta)
    g = jnp.cumsum(chunks(log_a), axis=-1)
    causal = jnp.tril(jnp.ones((CHUNK, CHUNK), dtype=bool))
    strict = jnp.tril(jnp.ones((CHUNK, CHUNK), dtype=bool), -1)
    decay = jnp.exp(jnp.where(causal, g[..., :, None] - g[..., None, :], -jnp.inf))

    kb = k * beta[..., None]
    kkt = jnp.einsum('bhnid,bhnjd->bhnij', kb, k) * decay
    a_mat = jnp.eye(CHUNK, dtype=jnp.float32) + jnp.where(strict, kkt, 0.0)
    rhs = jnp.concatenate([v * beta[..., None], kb * jnp.exp(g)[..., None]], axis=-1)
    sol = lax.linalg.triangular_solve(a_mat, rhs, left_side=True, lower=True,
                                      unit_diagonal=True)
    u = sol[..., :dv]
    w = sol[..., dv:]

    attn_intra = jnp.where(causal, jnp.einsum('bhnid,bhnjd->bhnij', q, k) * decay, 0.0)
    q_dec = q * jnp.exp(g)[..., None]
    g_last = g[..., -1]
    k_dec = k * jnp.exp(g_last[..., None] - g)[..., None]

    def step(state, inp):
        u_c, w_c, qd_c, kd_c, at_c, gl_c = inp
        v_new = u_c - jnp.einsum('bhck,bhkv->bhcv', w_c, state)
        o = jnp.einsum('bhck,bhkv->bhcv', qd_c, state) + jnp.einsum('bhcj,bhjv->bhcv', at_c, v_new)
        state = state * jnp.exp(gl_c)[..., None, None] + jnp.einsum('bhck,bhcv->bhkv', kd_c, v_new)
        return state, o

    xs = tuple(jnp.moveaxis(t, 2, 0) for t in (u, w, q_dec, k_dec, attn_intra, g_last))
    s0 = jnp.zeros((bsz, h, dk, dv), dtype=jnp.float32)
    _, o = lax.scan(step, s0, xs)
    return jnp.transpose(o, (1, 0, 3, 2, 4)).reshape(bsz, seq, h, dv)


def mixer(h, w_in, b_in, conv_dw_w, conv_dw_b, conv_ln_g, conv_ln_b, w_conv_proj, b_conv_proj,
          short_conv_w, a_log, dt_bias, gdn_norm_w, w_gdn_proj, w_out):
    bsz, seq, _ = h.shape
    z = h @ w_in + b_in
    offsets = np.cumsum(IN_SPLITS)[:-1].tolist()
    glu_a, glu_b, qkv, zgate, beta_raw, a_raw, gate_a, gate_b = jnp.split(z, offsets, axis=-1)

    c = glu_a * jax.nn.sigmoid(glu_b)
    c = causal_depthwise_conv(c, conv_dw_w) + conv_dw_b
    c = jax.nn.silu(layer_norm(c, conv_ln_g, conv_ln_b))
    y_conv = c @ w_conv_proj + b_conv_proj

    qkv = jax.nn.silu(causal_depthwise_conv(qkv, short_conv_w))
    q, k, v = jnp.split(qkv, [KEY_DIM, 2 * KEY_DIM], axis=-1)
    q = l2_normalize(q.reshape(bsz, seq, N_HEADS, HEAD_DIM_K).astype(jnp.float32))
    k = l2_normalize(k.reshape(bsz, seq, N_HEADS, HEAD_DIM_K).astype(jnp.float32))
    v = v.reshape(bsz, seq, N_HEADS, HEAD_DIM_V).astype(jnp.float32)
    beta = jax.nn.sigmoid(beta_raw.astype(jnp.float32))
    log_a = -jnp.exp(a_log.astype(jnp.float32)) * jax.nn.softplus(
        a_raw.astype(jnp.float32) + dt_bias.astype(jnp.float32))
    o = gated_delta_rule(q, k, v, beta, log_a)
    o = o * lax.rsqrt(jnp.mean(jnp.square(o), axis=-1, keepdims=True) + 1e-6)
    o = o * gdn_norm_w.astype(jnp.float32) * jax.nn.silu(
        zgate.reshape(bsz, seq, N_HEADS, HEAD_DIM_V).astype(jnp.float32))
    y_gdn = o.reshape(bsz, seq, VALUE_DIM).astype(h.dtype) @ w_gdn_proj

    m = jax.nn.sigmoid(gate_a) * y_conv + jax.nn.sigmoid(gate_b) * y_gdn
    return m @ w_out


def swiglu(h, w_ffn_in, w_ffn_out):
    gate, up = jnp.split(h @ w_ffn_in, 2, axis=-1)
    return (jax.nn.silu(gate) * up) @ w_ffn_out


def _fwd_setup_inputs(seed: int = 0) -> dict:
    key = jax.random.key(seed)
    ks = jax.random.split(key, 24)
    f32 = jnp.float32
    L = DEPTH

    def nrm(k, shape, scale):
        return jax.random.normal(k, shape, f32) * scale

    dt = jnp.exp(jax.random.uniform(ks[10], (L, N_HEADS), f32, np.log(1e-3), np.log(1e-1)))
    return {
        "x": jax.random.normal(ks[0], (BATCH, SEQ, D_MODEL), f32),
        "w_in": nrm(ks[1], (L, D_MODEL, IN_WIDTH), D_MODEL ** -0.5),
        "b_in": nrm(ks[2], (L, IN_WIDTH), 0.02),
        "conv_dw_w": nrm(ks[3], (L, CONV_WIDTH, CONV_CHANNELS), CONV_WIDTH ** -0.5),
        "conv_dw_b": nrm(ks[4], (L, CONV_CHANNELS), 0.02),
        "conv_ln_g": 1.0 + nrm(ks[5], (L, CONV_CHANNELS), 0.02),
        "conv_ln_b": nrm(ks[6], (L, CONV_CHANNELS), 0.02),
        "w_conv_proj": nrm(ks[7], (L, CONV_CHANNELS, D_MODEL), CONV_CHANNELS ** -0.5),
        "b_conv_proj": nrm(ks[8], (L, D_MODEL), 0.02),
        "short_conv_w": nrm(ks[9], (L, SHORT_CONV, 2 * KEY_DIM + VALUE_DIM), SHORT_CONV ** -0.5),
        "a_log": jnp.log(jax.random.uniform(ks[11], (L, N_HEADS), f32, 1.0, 16.0)),
        "dt_bias": dt + jnp.log(-jnp.expm1(-dt)),
        "gdn_norm_w": 1.0 + nrm(ks[12], (L, HEAD_DIM_V), 0.02),
        "w_gdn_proj": nrm(ks[13], (L, VALUE_DIM, D_MODEL), VALUE_DIM ** -0.5),
        "w_out": nrm(ks[14], (L, D_MODEL, D_MODEL), DEEPNORM_BETA * D_MODEL ** -0.5),
        "ln1_g": 1.0 + nrm(ks[15], (L, D_MODEL), 0.02),
        "ln1_b": nrm(ks[16], (L, D_MODEL), 0.02),
        "w_ffn_in": nrm(ks[17], (L, D_MODEL, 2 * D_FF), D_MODEL ** -0.5),
        "w_ffn_out": nrm(ks[18], (L, D_FF, D_MODEL), DEEPNORM_BETA * D_FF ** -0.5),
        "ln2_g": 1.0 + nrm(ks[19], (L, D_MODEL), 0.02),
        "ln2_b": nrm(ks[20], (L, D_MODEL), 0.02),
    }


def _fwd_reference(x, w_in, b_in, conv_dw_w, conv_dw_b, conv_ln_g, conv_ln_b, w_conv_proj, b_conv_proj,
              short_conv_w, a_log, dt_bias, gdn_norm_w, w_gdn_proj, w_out, ln1_g, ln1_b,
              w_ffn_in, w_ffn_out, ln2_g, ln2_b):
    for l in range(DEPTH):
        mix = mixer(x, w_in[l], b_in[l], conv_dw_w[l], conv_dw_b[l], conv_ln_g[l], conv_ln_b[l],
                    w_conv_proj[l], b_conv_proj[l], short_conv_w[l], a_log[l], dt_bias[l],
                    gdn_norm_w[l], w_gdn_proj[l], w_out[l])
        x = layer_norm(DEEPNORM_ALPHA * x + mix, ln1_g[l], ln1_b[l])
        x = layer_norm(DEEPNORM_ALPHA * x + swiglu(x, w_ffn_in[l], w_ffn_out[l]), ln2_g[l], ln2_b[l])
    return x


import jax as _jax
import jax.numpy as _jnp

TWIN_FORMAT = 'train_step'
FWD_PARAMS = ['x', 'w_in', 'b_in', 'conv_dw_w', 'conv_dw_b', 'conv_ln_g', 'conv_ln_b', 'w_conv_proj', 'b_conv_proj', 'short_conv_w', 'a_log', 'dt_bias', 'gdn_norm_w', 'w_gdn_proj', 'w_out', 'ln1_g', 'ln1_b', 'w_ffn_in', 'w_ffn_out', 'ln2_g', 'ln2_b']
TWIN_WEIGHTS = ['w_in', 'b_in', 'conv_dw_w', 'conv_dw_b', 'conv_ln_g', 'conv_ln_b', 'w_conv_proj', 'b_conv_proj', 'short_conv_w', 'a_log', 'dt_bias', 'gdn_norm_w', 'w_gdn_proj', 'w_out', 'ln1_g', 'ln1_b', 'w_ffn_in', 'w_ffn_out', 'ln2_g', 'ln2_b']
TWIN_DIFF_INPUT = 'x'
TWIN_INPUTS = ['x', 'w_in', 'b_in', 'conv_dw_w', 'conv_dw_b', 'conv_ln_g', 'conv_ln_b', 'w_conv_proj', 'b_conv_proj', 'short_conv_w', 'a_log', 'dt_bias', 'gdn_norm_w', 'w_gdn_proj', 'w_out', 'ln1_g', 'ln1_b', 'w_ffn_in', 'w_ffn_out', 'ln2_g', 'ln2_b', 'loss_target', 'm_w_in', 'm_b_in', 'm_conv_dw_w', 'm_conv_dw_b', 'm_conv_ln_g', 'm_conv_ln_b', 'm_w_conv_proj', 'm_b_conv_proj', 'm_short_conv_w', 'm_a_log', 'm_dt_bias', 'm_gdn_norm_w', 'm_w_gdn_proj', 'm_w_out', 'm_ln1_g', 'm_ln1_b', 'm_w_ffn_in', 'm_w_ffn_out', 'm_ln2_g', 'm_ln2_b', 'v_w_in', 'v_b_in', 'v_conv_dw_w', 'v_conv_dw_b', 'v_conv_ln_g', 'v_conv_ln_b', 'v_w_conv_proj', 'v_b_conv_proj', 'v_short_conv_w', 'v_a_log', 'v_dt_bias', 'v_gdn_norm_w', 'v_w_gdn_proj', 'v_w_out', 'v_ln1_g', 'v_ln1_b', 'v_w_ffn_in', 'v_w_ffn_out', 'v_ln2_g', 'v_ln2_b']
TWIN_OUTPUTS = ['loss', 'grad_x', 'grad_w_in', 'grad_b_in', 'grad_conv_dw_w', 'grad_conv_dw_b', 'grad_conv_ln_g', 'grad_conv_ln_b', 'grad_w_conv_proj', 'grad_b_conv_proj', 'grad_short_conv_w', 'grad_a_log', 'grad_dt_bias', 'grad_gdn_norm_w', 'grad_w_gdn_proj', 'grad_w_out', 'grad_ln1_g', 'grad_ln1_b', 'grad_w_ffn_in', 'grad_w_ffn_out', 'grad_ln2_g', 'grad_ln2_b', 'delta_w_in', 'delta_b_in', 'delta_conv_dw_w', 'delta_conv_dw_b', 'delta_conv_ln_g', 'delta_conv_ln_b', 'delta_w_conv_proj', 'delta_b_conv_proj', 'delta_short_conv_w', 'delta_a_log', 'delta_dt_bias', 'delta_gdn_norm_w', 'delta_w_gdn_proj', 'delta_w_out', 'delta_ln1_g', 'delta_ln1_b', 'delta_w_ffn_in', 'delta_w_ffn_out', 'delta_ln2_g', 'delta_ln2_b', 'new_m_w_in', 'new_m_b_in', 'new_m_conv_dw_w', 'new_m_conv_dw_b', 'new_m_conv_ln_g', 'new_m_conv_ln_b', 'new_m_w_conv_proj', 'new_m_b_conv_proj', 'new_m_short_conv_w', 'new_m_a_log', 'new_m_dt_bias', 'new_m_gdn_norm_w', 'new_m_w_gdn_proj', 'new_m_w_out', 'new_m_ln1_g', 'new_m_ln1_b', 'new_m_w_ffn_in', 'new_m_w_ffn_out', 'new_m_ln2_g', 'new_m_ln2_b', 'new_v_w_in', 'new_v_b_in', 'new_v_conv_dw_w', 'new_v_conv_dw_b', 'new_v_conv_ln_g', 'new_v_conv_ln_b', 'new_v_w_conv_proj', 'new_v_b_conv_proj', 'new_v_short_conv_w', 'new_v_a_log', 'new_v_dt_bias', 'new_v_gdn_norm_w', 'new_v_w_gdn_proj', 'new_v_w_out', 'new_v_ln1_g', 'new_v_ln1_b', 'new_v_w_ffn_in', 'new_v_w_ffn_out', 'new_v_ln2_g', 'new_v_ln2_b']
TWIN_LEAF_KINDS = {'loss': 'loss', 'grad_x': 'grad_x', 'grad_w_in': 'grad_w', 'grad_b_in': 'grad_w', 'grad_conv_dw_w': 'grad_w', 'grad_conv_dw_b': 'grad_w', 'grad_conv_ln_g': 'grad_w', 'grad_conv_ln_b': 'grad_w', 'grad_w_conv_proj': 'grad_w', 'grad_b_conv_proj': 'grad_w', 'grad_short_conv_w': 'grad_w', 'grad_a_log': 'grad_w', 'grad_dt_bias': 'grad_w', 'grad_gdn_norm_w': 'grad_w', 'grad_w_gdn_proj': 'grad_w', 'grad_w_out': 'grad_w', 'grad_ln1_g': 'grad_w', 'grad_ln1_b': 'grad_w', 'grad_w_ffn_in': 'grad_w', 'grad_w_ffn_out': 'grad_w', 'grad_ln2_g': 'grad_w', 'grad_ln2_b': 'grad_w', 'delta_w_in': 'delta_w', 'delta_b_in': 'delta_w', 'delta_conv_dw_w': 'delta_w', 'delta_conv_dw_b': 'delta_w', 'delta_conv_ln_g': 'delta_w', 'delta_conv_ln_b': 'delta_w', 'delta_w_conv_proj': 'delta_w', 'delta_b_conv_proj': 'delta_w', 'delta_short_conv_w': 'delta_w', 'delta_a_log': 'delta_w', 'delta_dt_bias': 'delta_w', 'delta_gdn_norm_w': 'delta_w', 'delta_w_gdn_proj': 'delta_w', 'delta_w_out': 'delta_w', 'delta_ln1_g': 'delta_w', 'delta_ln1_b': 'delta_w', 'delta_w_ffn_in': 'delta_w', 'delta_w_ffn_out': 'delta_w', 'delta_ln2_g': 'delta_w', 'delta_ln2_b': 'delta_w', 'new_m_w_in': 'new_m', 'new_m_b_in': 'new_m', 'new_m_conv_dw_w': 'new_m', 'new_m_conv_dw_b': 'new_m', 'new_m_conv_ln_g': 'new_m', 'new_m_conv_ln_b': 'new_m', 'new_m_w_conv_proj': 'new_m', 'new_m_b_conv_proj': 'new_m', 'new_m_short_conv_w': 'new_m', 'new_m_a_log': 'new_m', 'new_m_dt_bias': 'new_m', 'new_m_gdn_norm_w': 'new_m', 'new_m_w_gdn_proj': 'new_m', 'new_m_w_out': 'new_m', 'new_m_ln1_g': 'new_m', 'new_m_ln1_b': 'new_m', 'new_m_w_ffn_in': 'new_m', 'new_m_w_ffn_out': 'new_m', 'new_m_ln2_g': 'new_m', 'new_m_ln2_b': 'new_m', 'new_v_w_in': 'new_v', 'new_v_b_in': 'new_v', 'new_v_conv_dw_w': 'new_v', 'new_v_conv_dw_b': 'new_v', 'new_v_conv_ln_g': 'new_v', 'new_v_conv_ln_b': 'new_v', 'new_v_w_conv_proj': 'new_v', 'new_v_b_conv_proj': 'new_v', 'new_v_short_conv_w': 'new_v', 'new_v_a_log': 'new_v', 'new_v_dt_bias': 'new_v', 'new_v_gdn_norm_w': 'new_v', 'new_v_w_gdn_proj': 'new_v', 'new_v_w_out': 'new_v', 'new_v_ln1_g': 'new_v', 'new_v_ln1_b': 'new_v', 'new_v_w_ffn_in': 'new_v', 'new_v_w_ffn_out': 'new_v', 'new_v_ln2_g': 'new_v', 'new_v_ln2_b': 'new_v'}


def _forward(args):
    return _fwd_reference(*[args[k] for k in FWD_PARAMS])


def _output_shape():
    def fwd():
        inp = _fwd_setup_inputs(0)
        return _fwd_reference(*[inp[k] for k in FWD_PARAMS])
    out = _jax.eval_shape(fwd)
    return out.shape, out.dtype

N_MICROBATCH = 1
ADAM_LR = 0.001
ADAM_B1 = 0.9
ADAM_B2 = 0.999
ADAM_EPS = 1e-08
ADAM_WD = 0.01
ADAM_STEP = 10
PER_EXAMPLE_BATCH_AXIS = {'x': 0, 'loss_target': 0}
SHARED_INPUTS = []
_WEIGHT_DTYPES = {'w_in': _jnp.float32, 'b_in': _jnp.float32, 'conv_dw_w': _jnp.float32, 'conv_dw_b': _jnp.float32, 'conv_ln_g': _jnp.float32, 'conv_ln_b': _jnp.float32, 'w_conv_proj': _jnp.float32, 'b_conv_proj': _jnp.float32, 'short_conv_w': _jnp.float32, 'a_log': _jnp.float32, 'dt_bias': _jnp.float32, 'gdn_norm_w': _jnp.float32, 'w_gdn_proj': _jnp.float32, 'w_out': _jnp.float32, 'ln1_g': _jnp.float32, 'ln1_b': _jnp.float32, 'w_ffn_in': _jnp.float32, 'w_ffn_out': _jnp.float32, 'ln2_g': _jnp.float32, 'ln2_b': _jnp.float32}
MOMENT_SCALE = {'w_in': 7.674565e-03, 'b_in': 1.391859e-02, 'conv_dw_w': 1.118983e-02, 'conv_dw_b': 4.911781e-02, 'conv_ln_g': 2.124245e-02, 'conv_ln_b': 3.002614e-02, 'w_conv_proj': 1.413946e-02, 'b_conv_proj': 6.128198e-02, 'short_conv_w': 8.300393e-03, 'a_log': 4.107128e-02, 'dt_bias': 3.901354e-02, 'gdn_norm_w': 4.227634e-02, 'w_gdn_proj': 1.163067e-02, 'w_out': 4.343789e-02, 'ln1_g': 1.048853e+00, 'ln1_b': 4.963523e-01, 'w_ffn_in': 1.166580e-02, 'w_ffn_out': 4.530396e-02, 'ln2_g': 1.606733e+01, 'ln2_b': 1.262606e+00}


def _to_microbatches(a, axis):
    t = _jnp.moveaxis(a, axis, 0)
    t = t.reshape((N_MICROBATCH, t.shape[0] // N_MICROBATCH) + t.shape[1:])
    return _jnp.moveaxis(t, 1, axis + 1)


def setup_inputs(seed: int = 0) -> dict:
    inp = _fwd_setup_inputs(seed)
    key = _jax.random.fold_in(_jax.random.key(seed), 7919)
    shape, _ = _output_shape()
    out = dict(inp)
    out["loss_target"] = _jax.random.normal(_jax.random.fold_in(key, 0), shape, _jnp.float32)
    for i, name in enumerate(TWIN_WEIGHTS):
        w = inp[name].astype(_jnp.float32)
        if MOMENT_SCALE is None:
            s = _jnp.sqrt(_jnp.mean(_jnp.square(w)) + 1e-30)
        else:
            s = MOMENT_SCALE[name]
        km, kv = _jax.random.split(_jax.random.fold_in(key, i + 1))
        out[name] = w
        out["m_" + name] = s * _jax.random.normal(km, w.shape, _jnp.float32)
        out["v_" + name] = (s * s) * _jax.random.uniform(kv, w.shape, _jnp.float32, 0.5, 1.5)
    if N_MICROBATCH > 1:
        for name, axis in PER_EXAMPLE_BATCH_AXIS.items():
            out[name] = _to_microbatches(out[name], axis)
    return {'x': out['x'], 'w_in': out['w_in'], 'b_in': out['b_in'], 'conv_dw_w': out['conv_dw_w'], 'conv_dw_b': out['conv_dw_b'], 'conv_ln_g': out['conv_ln_g'], 'conv_ln_b': out['conv_ln_b'], 'w_conv_proj': out['w_conv_proj'], 'b_conv_proj': out['b_conv_proj'], 'short_conv_w': out['short_conv_w'], 'a_log': out['a_log'], 'dt_bias': out['dt_bias'], 'gdn_norm_w': out['gdn_norm_w'], 'w_gdn_proj': out['w_gdn_proj'], 'w_out': out['w_out'], 'ln1_g': out['ln1_g'], 'ln1_b': out['ln1_b'], 'w_ffn_in': out['w_ffn_in'], 'w_ffn_out': out['w_ffn_out'], 'ln2_g': out['ln2_g'], 'ln2_b': out['ln2_b'], 'loss_target': out['loss_target'], 'm_w_in': out['m_w_in'], 'm_b_in': out['m_b_in'], 'm_conv_dw_w': out['m_conv_dw_w'], 'm_conv_dw_b': out['m_conv_dw_b'], 'm_conv_ln_g': out['m_conv_ln_g'], 'm_conv_ln_b': out['m_conv_ln_b'], 'm_w_conv_proj': out['m_w_conv_proj'], 'm_b_conv_proj': out['m_b_conv_proj'], 'm_short_conv_w': out['m_short_conv_w'], 'm_a_log': out['m_a_log'], 'm_dt_bias': out['m_dt_bias'], 'm_gdn_norm_w': out['m_gdn_norm_w'], 'm_w_gdn_proj': out['m_w_gdn_proj'], 'm_w_out': out['m_w_out'], 'm_ln1_g': out['m_ln1_g'], 'm_ln1_b': out['m_ln1_b'], 'm_w_ffn_in': out['m_w_ffn_in'], 'm_w_ffn_out': out['m_w_ffn_out'], 'm_ln2_g': out['m_ln2_g'], 'm_ln2_b': out['m_ln2_b'], 'v_w_in': out['v_w_in'], 'v_b_in': out['v_b_in'], 'v_conv_dw_w': out['v_conv_dw_w'], 'v_conv_dw_b': out['v_conv_dw_b'], 'v_conv_ln_g': out['v_conv_ln_g'], 'v_conv_ln_b': out['v_conv_ln_b'], 'v_w_conv_proj': out['v_w_conv_proj'], 'v_b_conv_proj': out['v_b_conv_proj'], 'v_short_conv_w': out['v_short_conv_w'], 'v_a_log': out['v_a_log'], 'v_dt_bias': out['v_dt_bias'], 'v_gdn_norm_w': out['v_gdn_norm_w'], 'v_w_gdn_proj': out['v_w_gdn_proj'], 'v_w_out': out['v_w_out'], 'v_ln1_g': out['v_ln1_g'], 'v_ln1_b': out['v_ln1_b'], 'v_w_ffn_in': out['v_w_ffn_in'], 'v_w_ffn_out': out['v_w_ffn_out'], 'v_ln2_g': out['v_ln2_g'], 'v_ln2_b': out['v_ln2_b']}


def _loss(weights, diff, rest, loss_target):
    with _jax.named_scope("forward"):
        args = {**rest, TWIN_DIFF_INPUT: diff, **{k: w.astype(_WEIGHT_DTYPES[k]) for k, w in weights.items()}}
        y = _forward(args)
    with _jax.named_scope("loss_head"):
        err = _jnp.square(y.astype(_jnp.float32) - loss_target)
        return 0.5 * _jnp.sum(_jnp.mean(err, axis=-1)) if err.ndim else 0.5 * err


def _adamw(w, g, m, v):
    m = ADAM_B1 * m + (1.0 - ADAM_B1) * g
    v = ADAM_B2 * v + (1.0 - ADAM_B2) * _jnp.square(g)
    m_hat = m / (1.0 - ADAM_B1 ** ADAM_STEP)
    v_hat = v / (1.0 - ADAM_B2 ** ADAM_STEP)
    delta = -ADAM_LR * (m_hat / (_jnp.sqrt(v_hat) + ADAM_EPS) + ADAM_WD * w)
    return delta, m, v


def reference(x, w_in, b_in, conv_dw_w, conv_dw_b, conv_ln_g, conv_ln_b, w_conv_proj, b_conv_proj, short_conv_w, a_log, dt_bias, gdn_norm_w, w_gdn_proj, w_out, ln1_g, ln1_b, w_ffn_in, w_ffn_out, ln2_g, ln2_b, loss_target, m_w_in, m_b_in, m_conv_dw_w, m_conv_dw_b, m_conv_ln_g, m_conv_ln_b, m_w_conv_proj, m_b_conv_proj, m_short_conv_w, m_a_log, m_dt_bias, m_gdn_norm_w, m_w_gdn_proj, m_w_out, m_ln1_g, m_ln1_b, m_w_ffn_in, m_w_ffn_out, m_ln2_g, m_ln2_b, v_w_in, v_b_in, v_conv_dw_w, v_conv_dw_b, v_conv_ln_g, v_conv_ln_b, v_w_conv_proj, v_b_conv_proj, v_short_conv_w, v_a_log, v_dt_bias, v_gdn_norm_w, v_w_gdn_proj, v_w_out, v_ln1_g, v_ln1_b, v_w_ffn_in, v_w_ffn_out, v_ln2_g, v_ln2_b):
    given = dict(x=x, w_in=w_in, b_in=b_in, conv_dw_w=conv_dw_w, conv_dw_b=conv_dw_b, conv_ln_g=conv_ln_g, conv_ln_b=conv_ln_b, w_conv_proj=w_conv_proj, b_conv_proj=b_conv_proj, short_conv_w=short_conv_w, a_log=a_log, dt_bias=dt_bias, gdn_norm_w=gdn_norm_w, w_gdn_proj=w_gdn_proj, w_out=w_out, ln1_g=ln1_g, ln1_b=ln1_b, w_ffn_in=w_ffn_in, w_ffn_out=w_ffn_out, ln2_g=ln2_g, ln2_b=ln2_b, loss_target=loss_target, m_w_in=m_w_in, m_b_in=m_b_in, m_conv_dw_w=m_conv_dw_w, m_conv_dw_b=m_conv_dw_b, m_conv_ln_g=m_conv_ln_g, m_conv_ln_b=m_conv_ln_b, m_w_conv_proj=m_w_conv_proj, m_b_conv_proj=m_b_conv_proj, m_short_conv_w=m_short_conv_w, m_a_log=m_a_log, m_dt_bias=m_dt_bias, m_gdn_norm_w=m_gdn_norm_w, m_w_gdn_proj=m_w_gdn_proj, m_w_out=m_w_out, m_ln1_g=m_ln1_g, m_ln1_b=m_ln1_b, m_w_ffn_in=m_w_ffn_in, m_w_ffn_out=m_w_ffn_out, m_ln2_g=m_ln2_g, m_ln2_b=m_ln2_b, v_w_in=v_w_in, v_b_in=v_b_in, v_conv_dw_w=v_conv_dw_w, v_conv_dw_b=v_conv_dw_b, v_conv_ln_g=v_conv_ln_g, v_conv_ln_b=v_conv_ln_b, v_w_conv_proj=v_w_conv_proj, v_b_conv_proj=v_b_conv_proj, v_short_conv_w=v_short_conv_w, v_a_log=v_a_log, v_dt_bias=v_dt_bias, v_gdn_norm_w=v_gdn_norm_w, v_w_gdn_proj=v_w_gdn_proj, v_w_out=v_w_out, v_ln1_g=v_ln1_g, v_ln1_b=v_ln1_b, v_w_ffn_in=v_w_ffn_in, v_w_ffn_out=v_w_ffn_out, v_ln2_g=v_ln2_g, v_ln2_b=v_ln2_b)
    weights = {n: given[n] for n in TWIN_WEIGHTS}
    shared = {n: given[n] for n in SHARED_INPUTS}
    per_example = {n: given[n] for n in ['x']}
    grad_fn = _jax.value_and_grad(_loss, argnums=(0, 1))

    def one_microbatch(ex, loss_target):
        ex = dict(ex)
        diff = ex.pop(TWIN_DIFF_INPUT)
        return grad_fn(weights, diff, {**shared, **ex}, loss_target)

    if N_MICROBATCH == 1:
        loss, (grad_w, grad_x) = one_microbatch(per_example, given["loss_target"])
    else:
        def body(carry, xs):
            loss_sum, grad_sum = carry
            l_k, (gw_k, gx_k) = one_microbatch(xs[0], xs[1])
            with _jax.named_scope("update"):
                return (loss_sum + l_k, _jax.tree.map(_jnp.add, grad_sum, gw_k)), gx_k

        init = (_jnp.zeros((), _jnp.float32), _jax.tree.map(_jnp.zeros_like, weights))
        (loss, grad_w), grad_x = _jax.lax.scan(body, init, (per_example, given["loss_target"]))
    with _jax.named_scope("update"):
        delta_w, new_m, new_v = {}, {}, {}
        for n in TWIN_WEIGHTS:
            delta_w[n], new_m[n], new_v[n] = _adamw(weights[n], grad_w[n], given["m_" + n], given["v_" + n])
    return (loss, grad_x, *[grad_w[n] for n in TWIN_WEIGHTS], *[delta_w[n] for n in TWIN_WEIGHTS],
            *[new_m[n] for n in TWIN_WEIGHTS], *[new_v[n] for n in TWIN_WEIGHTS])
```

```python
import functools
import math

import jax
import jax.numpy as jnp
from jax import lax
from jax.experimental import pallas as pl
from jax.experimental.pallas import tpu as pltpu

f32, bf16 = jnp.float32, jnp.bfloat16
SDS = jax.ShapeDtypeStruct
MESH = pl.DeviceIdType.MESH
AXES = ("x", "y", "c")
N_DEV = 8

CONV_WIDTH = 31
SHORT_CONV = 4
HEAD_DIM = 128
CHUNK = 64
LN_EPS = 1e-5
ADAM_LR, ADAM_B1, ADAM_B2, ADAM_EPS, ADAM_WD, ADAM_STEP = 0.001, 0.9, 0.999, 1e-08, 0.01, 10

LANES = 128
HALO = 32
PACK_COLS = 1024
VMEM_LIMIT = 56 * 1024 * 1024


def _pcall(body, **kw):
    return pl.pallas_call(body, **kw)


def _cparams(sem=None):
    return pltpu.CompilerParams(dimension_semantics=sem, vmem_limit_bytes=VMEM_LIMIT)


def _pick(n, cap, mult):
    if n <= cap:
        return n
    for t in range(cap - cap % mult, 0, -mult):
        if n % t == 0:
            return t
    raise ValueError(f"no tile for {n} under {cap} in steps of {mult}")


def matmul(a, b, mode, name, bias=None, out_dtype=f32, tm_cap=1024, tn_cap=1280, tk_cap=2048):
    if mode == "nn":
        (M, K), (K2, N) = a.shape, b.shape
    elif mode == "nt":
        (M, K), (N, K2) = a.shape, b.shape
    else:
        (K, M), (K2, N) = a.shape, b.shape
    assert K == K2, (a.shape, b.shape, mode)
    tm = _pick(M, tm_cap, 256 if M % 256 == 0 else 8)
    tn = _pick(N, tn_cap, 256 if N % 256 == 0 else LANES)
    tk = _pick(K, tk_cap, 256 if K % 256 == 0 else LANES)
    nm, nn, nk = M // tm, N // tn, K // tk
    dims = {"nn": (((1,), (0,)), ((), ())), "nt": (((1,), (1,)), ((), ())), "tn": (((0,), (0,)), ((), ()))}[mode]
    has_bias = bias is not None

    def body(*refs):
        a_ref, b_ref = refs[0], refs[1]
        bias_ref = refs[2] if has_bias else None
        o_ref = refs[2 + has_bias]
        prod = lax.dot_general(a_ref[...], b_ref[...], dims, preferred_element_type=f32)

        def finish(acc):
            if has_bias:
                acc = acc + bias_ref[...]
            o_ref[...] = acc.astype(out_dtype)

        if nk == 1:
            finish(prod)
        else:
            acc_ref = refs[3 + has_bias]
            k = pl.program_id(2)

            @pl.when(k == 0)
            def _():
                acc_ref[...] = prod

            @pl.when(jnp.logical_and(k > 0, k < nk - 1))
            def _():
                acc_ref[...] += prod

            @pl.when(k == nk - 1)
            def _():
                finish(acc_ref[...] + prod)

    a_bytes, b_bytes = M * K, N * K
    m_outer = a_bytes >= b_bytes
    if m_outer:
        grid = (nm, nn, nk)
        gi = lambda i, j, k: (i, j, k)
    else:
        grid = (nn, nm, nk)
        gi = lambda j, i, k: (i, j, k)

    def amap(*g):
        i, j, k = gi(*g)
        return (k, i) if mode == "tn" else (i, k)

    def bmap(*g):
        i, j, k = gi(*g)
        return (j, k) if mode == "nt" else (k, j)

    def omap(*g):
        i, j, k = gi(*g)
        return (i, j)

    def biasmap(*g):
        i, j, k = gi(*g)
        return (0, j)

    in_specs = [pl.BlockSpec((tk, tm) if mode == "tn" else (tm, tk), amap),
                pl.BlockSpec((tn, tk) if mode == "nt" else (tk, tn), bmap)]
    args = [a, b]
    if has_bias:
        in_specs.append(pl.BlockSpec((1, tn), biasmap))
        args.append(bias.reshape(1, N).astype(f32))
    return _pcall(body, out_shape=SDS((M, N), out_dtype), grid=grid, in_specs=in_specs,
                  out_specs=pl.BlockSpec((tm, tn), omap),
                  scratch_shapes=[pltpu.VMEM((tm, tn), f32)] if nk > 1 else [],
                  compiler_params=_cparams(("parallel", "parallel", "arbitrary")), name=name)(*args)


def _row_specs(rows, tt):
    specs, args = [], []
    for arr, cb, width in rows:
        specs.append(pl.BlockSpec((tt, width), lambda i, cb=cb: (i, cb)))
        args.append(arr)
    return specs, args


def _param_specs(params):
    return [pl.BlockSpec(p.shape, lambda i: (0, 0)) for p in params]


def rowwise(fn, rows, params, out_dtypes, tt, name):
    T = rows[0][0].shape[0]
    nr, npar = len(rows), len(params)
    blocks = [SDS((tt, w), f32) for _, _, w in rows] + [SDS(p.shape, f32) for p in params]
    outs = jax.eval_shape(fn, *blocks)
    out_shape, out_specs = [], []
    for o, dts in zip(outs, out_dtypes):
        for dt in dts:
            out_shape.append(SDS((T, o.shape[1]), dt))
            out_specs.append(pl.BlockSpec((tt, o.shape[1]), lambda i: (i, 0)))

    def body(*refs):
        xs = [r[...].astype(f32) for r in refs[:nr + npar]]
        res = fn(*xs)
        k = nr + npar
        for o, dts in zip(res, out_dtypes):
            for dt in dts:
                refs[k][...] = o.astype(dt)
                k += 1

    rspecs, rargs = _row_specs(rows, tt)
    return _pcall(body, out_shape=out_shape, grid=(T // tt,), in_specs=rspecs + _param_specs(params),
                  out_specs=out_specs, compiler_params=_cparams(("parallel",)), name=name)(*rargs, *params)


def rowwise_vjp(fn, rows, params, cots, d_dtypes, tt, name):
    T = rows[0][0].shape[0]
    nr, npar = len(rows), len(params)
    ncot = [len(c) for c in cots]
    flat_cots = [c for cs in cots for c in cs]

    def body(*refs):
        i = pl.program_id(0)
        xs = [r[...].astype(f32) for r in refs[:nr + npar]]
        k = nr + npar
        cs = []
        for n in ncot:
            tot = refs[k][...].astype(f32)
            for r in refs[k + 1:k + n]:
                tot = tot + r[...].astype(f32)
            cs.append(tot)
            k += n
        _, pull = jax.vjp(fn, *xs)
        grads = pull(tuple(cs))
        for g, dt in zip(grads[:nr], d_dtypes):
            refs[k][...] = g.astype(dt)
            k += 1
        for g in grads[nr:]:
            ref = refs[k]
            k += 1

            @pl.when(i == 0)
            def _(ref=ref, g=g):
                ref[...] = g

            @pl.when(i > 0)
            def _(ref=ref, g=g):
                ref[...] += g

    rspecs, rargs = _row_specs(rows, tt)
    cot_specs = [pl.BlockSpec((tt, c.shape[1]), lambda i: (i, 0)) for c in flat_cots]
    out_shape = [SDS((T, w), dt) for (_, _, w), dt in zip(rows, d_dtypes)] + [SDS(p.shape, f32) for p in params]
    out_specs = [pl.BlockSpec((tt, w), lambda i: (i, 0)) for _, _, w in rows] + _param_specs(params)
    return _pcall(body, out_shape=out_shape, grid=(T // tt,), in_specs=rspecs + _param_specs(params) + cot_specs,
                  out_specs=out_specs, compiler_params=_cparams(("arbitrary",)), name=name)(*rargs, *params, *flat_cots)


def _sigmoid(x):
    return jax.nn.sigmoid(x)


def _silu(x):
    return x * jax.nn.sigmoid(x)


def _softplus(x):
    return jnp.maximum(x, 0.0) + jnp.log(1.0 + jnp.exp(-jnp.abs(x)))


def _layer_norm(x, g, b):
    mu = jnp.mean(x, axis=-1, keepdims=True)
    xc = x - mu
    var = jnp.mean(xc * xc, axis=-1, keepdims=True)
    return xc * lax.rsqrt(var + LN_EPS) * g + b


def glu_fn(a, b):
    return (a * _sigmoid(b),)


def lnsilu_fn(c, g, b):
    return (_silu(_layer_norm(c, g, b)),)


def merge_fn(ga, gb, yc, yg):
    return (_sigmoid(ga) * yc + _sigmoid(gb) * yg,)


def swiglu_fn(gate, up):
    return (_silu(gate) * up,)


def make_lnres_fn(alpha):
    def lnres_fn(h, y, g, b):
        return (_layer_norm(alpha * h + y, g, b),)
    return lnres_fn


def add_fn(a, b):
    return (a + b,)


def conv_fwd(x, x_col0, w, w_col0, width, K, bias, name, tt=512, tc=256):
    T = x.shape[0]
    tt = min(tt, T)
    assert width % tc == 0 and x_col0 % tc == 0 and w_col0 % tc == 0 and tt % HALO == 0
    hb = tt // HALO
    xb, wb = x_col0 // tc, w_col0 // tc
    has_bias = bias is not None

    def body(*refs):
        x_ref, halo_ref, w_ref = refs[:3]
        b_ref = refs[3] if has_bias else None
        o_ref, ext_ref = refs[3 + has_bias], refs[4 + has_bias]
        i = pl.program_id(1)
        ext_ref[pl.ds(0, HALO), :] = jnp.where(i > 0, halo_ref[...], 0.0)
        ext_ref[pl.ds(HALO, tt), :] = x_ref[...]
        acc = jnp.zeros((tt, tc), f32)
        for j in range(K):
            acc = acc + w_ref[j:j + 1, :] * ext_ref[pl.ds(HALO - (K - 1) + j, tt), :]
        if has_bias:
            acc = acc + b_ref[...]
        o_ref[...] = acc

    in_specs = [pl.BlockSpec((tt, tc), lambda cb, i: (i, xb + cb)),
                pl.BlockSpec((HALO, tc), lambda cb, i: (jnp.maximum(i * hb - 1, 0), xb + cb)),
                pl.BlockSpec((K, tc), lambda cb, i: (0, wb + cb))]
    args = [x, x, w]
    if has_bias:
        in_specs.append(pl.BlockSpec((1, tc), lambda cb, i: (0, cb)))
        args.append(bias)
    return _pcall(body, out_shape=SDS((T, width), f32), grid=(width // tc, T // tt), in_specs=in_specs,
                  out_specs=pl.BlockSpec((tt, tc), lambda cb, i: (i, cb)),
                  scratch_shapes=[pltpu.VMEM((tt + HALO, tc), f32)],
                  compiler_params=_cparams(("parallel", "arbitrary")), name=name)(*args)


def conv_bwd(dy, x, x_col0, w, w_col0, width, K, dx_dtype, name, tt=512, tc=256):
    T = x.shape[0]
    tt = min(tt, T)
    hb = tt // HALO
    nt = T // tt
    xb, wb = x_col0 // tc, w_col0 // tc

    def body(dy_ref, dyn_ref, x_ref, halo_ref, w_ref, dx_ref, dw_ref, db_ref, xext_ref, dyext_ref):
        i = pl.program_id(1)
        xext_ref[pl.ds(0, HALO), :] = jnp.where(i > 0, halo_ref[...], 0.0)
        xext_ref[pl.ds(HALO, tt), :] = x_ref[...]
        dy = dy_ref[...].astype(f32)
        dyext_ref[pl.ds(0, tt), :] = dy
        dyext_ref[pl.ds(tt, HALO), :] = jnp.where(i < nt - 1, dyn_ref[...].astype(f32), 0.0)
        acc = jnp.zeros((tt, tc), f32)
        dws = []
        for j in range(K):
            acc = acc + w_ref[j:j + 1, :] * dyext_ref[pl.ds(K - 1 - j, tt), :]
            dws.append(jnp.sum(dy * xext_ref[pl.ds(HALO - (K - 1) + j, tt), :], axis=0, keepdims=True))
        dx_ref[...] = acc.astype(dx_dtype)
        dw = jnp.concatenate(dws, axis=0)
        db = jnp.sum(dy, axis=0, keepdims=True)

        @pl.when(i == 0)
        def _():
            dw_ref[...] = dw
            db_ref[...] = db

        @pl.when(i > 0)
        def _():
            dw_ref[...] += dw
            db_ref[...] += db

    in_specs = [pl.BlockSpec((tt, tc), lambda cb, i: (i, cb)),
                pl.BlockSpec((HALO, tc), lambda cb, i: (jnp.minimum((i + 1) * hb, nt * hb - 1), cb)),
                pl.BlockSpec((tt, tc), lambda cb, i: (i, xb + cb)),
                pl.BlockSpec((HALO, tc), lambda cb, i: (jnp.maximum(i * hb - 1, 0), xb + cb)),
                pl.BlockSpec((K, tc), lambda cb, i: (0, wb + cb))]
    out_shape = [SDS((T, width), dx_dtype), SDS((K, width), f32), SDS((1, width), f32)]
    out_specs = [pl.BlockSpec((tt, tc), lambda cb, i: (i, cb)), pl.BlockSpec((K, tc), lambda cb, i: (0, cb)),
                 pl.BlockSpec((1, tc), lambda cb, i: (0, cb))]
    return _pcall(body, out_shape=out_shape, grid=(width // tc, nt), in_specs=in_specs, out_specs=out_specs,
                  scratch_shapes=[pltpu.VMEM((tt + HALO, tc), f32), pltpu.VMEM((tt + HALO, tc), f32)],
                  compiler_params=_cparams(("parallel", "arbitrary")), name=name)(dy, dy, x, x, w)


def _mm_b(eq, a, b):
    return jnp.einsum(eq, a.astype(bf16), b.astype(bf16), preferred_element_type=f32)


def _mm_hi(eq, a, b):
    return jnp.einsum(eq, a, b, precision=lax.Precision.HIGHEST, preferred_element_type=f32)


def _gdn_chunk(n_heads, head0, S, qr, kr, vr, zg, zs, alog, dtb, nw):
    HB, C, dk = qr.shape
    q = _silu(qr)
    k = _silu(kr)
    v = _silu(vr)
    q = q * lax.rsqrt(jnp.sum(q * q, axis=-1, keepdims=True) + 1e-6) * (dk ** -0.5)
    k = k * lax.rsqrt(jnp.sum(k * k, axis=-1, keepdims=True) + 1e-6)
    beta_all = _sigmoid(zs)
    la_all = -jnp.exp(alog) * _softplus(zs + dtb)
    lane = lax.broadcasted_iota(jnp.int32, (C, LANES), 1)
    betas, las = [], []
    for h in range(HB):
        betas.append(jnp.sum(jnp.where(lane == head0 + h, beta_all, 0.0), axis=-1, keepdims=True))
        las.append(jnp.sum(jnp.where(lane == n_heads + head0 + h, la_all, 0.0), axis=-1, keepdims=True))
    beta = jnp.stack(betas, axis=0)
    la = jnp.stack(las, axis=0)
    ii = lax.broadcasted_iota(jnp.int32, (C, C), 0)
    jj = lax.broadcasted_iota(jnp.int32, (C, C), 1)
    eye = (ii == jj).astype(f32)[None]
    causal = (jj <= ii)[None]
    strict = (jj < ii)[None]
    la_row = jnp.sum(la * eye, axis=1, keepdims=True)
    g_col = jnp.sum(jnp.where(causal, la_row, 0.0), axis=2, keepdims=True)
    g_row = jnp.sum(jnp.where((ii <= jj)[None], la, 0.0), axis=1, keepdims=True)
    g_last = jnp.sum(la, axis=1, keepdims=True)
    decay = jnp.where(causal, jnp.exp(jnp.where(causal, g_col - g_row, 0.0)), 0.0)
    kb = k * beta
    L = jnp.where(strict, _mm_b("hid,hjd->hij", kb, k) * decay, 0.0)
    P = -L
    inv = eye + P
    span = 2
    while span < C:
        P = _mm_hi("hij,hjk->hik", P, P)
        inv = inv + _mm_hi("hij,hjk->hik", inv, P)
        span *= 2
    eg = jnp.exp(g_col)
    u = _mm_hi("hij,hjd->hid", inv, v * beta)
    w = _mm_hi("hij,hjd->hid", inv, kb * eg)
    attn = jnp.where(causal, _mm_b("hid,hjd->hij", q, k) * decay, 0.0)
    qd = q * eg
    kd = k * jnp.exp(g_last - g_col)
    v_new = u - _mm_b("hck,hkv->hcv", w, S)
    o = _mm_b("hck,hkv->hcv", qd, S) + _mm_b("hcj,hjv->hcv", attn, v_new)
    S_new = S * jnp.exp(g_last) + _mm_b("hck,hcv->hkv", kd, v_new)
    o = o * lax.rsqrt(jnp.mean(o * o, axis=-1, keepdims=True) + 1e-6) * nw * _silu(zg)
    return S_new, o


def _split_heads(x, HB):
    return jnp.stack([x[:, h * HEAD_DIM:(h + 1) * HEAD_DIM] for h in range(HB)], axis=0)


def _merge_heads(x):
    return jnp.concatenate([x[h] for h in range(x.shape[0])], axis=-1)


def _gdn_dims(T, D):
    H = D // HEAD_DIM
    HB = min(4, H)
    tt = min(512, T)
    return H, HB, tt, tt // CHUNK


def gdn_fwd(qkv, z, zg_col0, zs_col0, alog_vec, dtb_vec, nw, name):
    T, D3 = qkv.shape
    D = D3 // 3
    H, HB, tt, nc = _gdn_dims(T, D)
    W = HB * HEAD_DIM
    nhg = H // HB
    zgb, zsb = zg_col0 // W, zs_col0 // LANES

    def body(q_ref, k_ref, v_ref, zg_ref, zs_ref, alog_ref, dtb_ref, nw_ref, o_ref, ssave_ref, s_ref):
        i, hg = pl.program_id(0), pl.program_id(1)
        hsl = pl.ds(hg * HB, HB)

        @pl.when(i == 0)
        def _():
            s_ref[hsl] = jnp.zeros((HB, HEAD_DIM, HEAD_DIM), f32)

        alog, dtb, nwv = alog_ref[...], dtb_ref[...], nw_ref[...]

        def step(c, carry):
            rows = pl.ds(pl.multiple_of(c * CHUNK, CHUNK), CHUNK)
            S = s_ref[hsl]
            ssave_ref[:, pl.ds(c, 1)] = S[:, None]
            S_new, o = _gdn_chunk(H, hg * HB, S, _split_heads(q_ref[rows, :], HB), _split_heads(k_ref[rows, :], HB),
                                  _split_heads(v_ref[rows, :], HB), _split_heads(zg_ref[rows, :], HB),
                                  zs_ref[rows, :], alog, dtb, nwv)
            s_ref[hsl] = S_new
            o_ref[rows, :] = _merge_heads(o).astype(bf16)
            return carry

        lax.fori_loop(0, nc, step, 0)

    col = lambda off: pl.BlockSpec((tt, W), lambda i, hg, off=off: (i, off + hg))
    in_specs = [col(0), col(nhg), col(2 * nhg), col(zgb), pl.BlockSpec((tt, LANES), lambda i, hg: (i, zsb)),
                pl.BlockSpec((1, LANES), lambda i, hg: (0, 0)), pl.BlockSpec((1, LANES), lambda i, hg: (0, 0)),
                pl.BlockSpec((1, LANES), lambda i, hg: (0, 0))]
    out_shape = [SDS((T, D), bf16), SDS((H, T // CHUNK, HEAD_DIM, HEAD_DIM), f32)]
    out_specs = [pl.BlockSpec((tt, W), lambda i, hg: (i, hg)),
                 pl.BlockSpec((HB, nc, HEAD_DIM, HEAD_DIM), lambda i, hg: (hg, i, 0, 0))]
    return _pcall(body, out_shape=out_shape, grid=(T // tt, nhg), in_specs=in_specs, out_specs=out_specs,
                  scratch_shapes=[pltpu.VMEM((H, HEAD_DIM, HEAD_DIM), f32)],
                  compiler_params=_cparams(("arbitrary", "arbitrary")), name=name)(qkv, qkv, qkv, z, z, alog_vec, dtb_vec, nw)


def gdn_bwd(do, qkv, z, zg_col0, zs_col0, alog_vec, dtb_vec, nw, ssave, name):
    T, D3 = qkv.shape
    D = D3 // 3
    H, HB, tt, nc = _gdn_dims(T, D)
    W = HB * HEAD_DIM
    nhg = H // HB
    nt = T // tt
    zgb, zsb = zg_col0 // W, zs_col0 // LANES

    def body(do_ref, q_ref, k_ref, v_ref, zg_ref, zs_ref, alog_ref, dtb_ref, nw_ref, ssave_ref,
             dq_ref, dk_ref, dv_ref, dzg_ref, dzs_ref, dalog_ref, ddtb_ref, dnw_ref, ds_ref):
        i, hg = pl.program_id(0), pl.program_id(1)
        hsl = pl.ds(hg * HB, HB)

        @pl.when(i == 0)
        def _():
            ds_ref[hsl] = jnp.zeros((HB, HEAD_DIM, HEAD_DIM), f32)

        @pl.when(jnp.logical_and(i == 0, hg == 0))
        def _():
            dalog_ref[...] = jnp.zeros_like(dalog_ref)
            ddtb_ref[...] = jnp.zeros_like(ddtb_ref)
            dnw_ref[...] = jnp.zeros_like(dnw_ref)

        @pl.when(hg == 0)
        def _():
            dzs_ref[...] = jnp.zeros_like(dzs_ref)

        alog, dtb, nwv = alog_ref[...], dtb_ref[...], nw_ref[...]

        def step(cc, carry):
            c = nc - 1 - cc
            rows = pl.ds(pl.multiple_of(c * CHUNK, CHUNK), CHUNK)
            S = ssave_ref[:, pl.ds(c, 1)][:, 0]
            fn = functools.partial(_gdn_chunk, H, hg * HB)
            _, pull = jax.vjp(fn, S, _split_heads(q_ref[rows, :], HB), _split_heads(k_ref[rows, :], HB),
                              _split_heads(v_ref[rows, :], HB), _split_heads(zg_ref[rows, :], HB),
                              zs_ref[rows, :], alog, dtb, nwv)
            dS, dq, dk, dv, dzg, dzs, dal, ddt, dnw = pull((ds_ref[hsl], _split_heads(do_ref[rows, :], HB)))
            ds_ref[hsl] = dS
            dq_ref[rows, :] = _merge_heads(dq)
            dk_ref[rows, :] = _merge_heads(dk)
            dv_ref[rows, :] = _merge_heads(dv)
            dzg_ref[rows, :] = _merge_heads(dzg).astype(bf16)
            dzs_ref[rows, :] += dzs
            dalog_ref[...] += dal
            ddtb_ref[...] += ddt
            dnw_ref[...] += dnw
            return carry

        lax.fori_loop(0, nc, step, 0)

    rev = lambda i: nt - 1 - i
    col = lambda off: pl.BlockSpec((tt, W), lambda i, hg, off=off: (rev(i), off + hg))
    vec = lambda r: pl.BlockSpec((r, LANES), lambda i, hg: (0, 0))
    in_specs = [col(0), col(0), col(nhg), col(2 * nhg), col(zgb), pl.BlockSpec((tt, LANES), lambda i, hg: (rev(i), zsb)),
                vec(1), vec(1), vec(1),
                pl.BlockSpec((HB, nc, HEAD_DIM, HEAD_DIM), lambda i, hg: (hg, rev(i), 0, 0))]
    out_shape = [SDS((T, D), f32), SDS((T, D), f32), SDS((T, D), f32), SDS((T, D), bf16), SDS((T, LANES), f32),
                 SDS((1, LANES), f32), SDS((1, LANES), f32), SDS((1, LANES), f32)]
    out_specs = [col(0), col(0), col(0), col(0), pl.BlockSpec((tt, LANES), lambda i, hg: (rev(i), 0)),
                 vec(1), vec(1), vec(1)]
    return _pcall(body, out_shape=out_shape, grid=(nt, nhg), in_specs=in_specs, out_specs=out_specs,
                  scratch_shapes=[pltpu.VMEM((H, HEAD_DIM, HEAD_DIM), f32)],
                  compiler_params=_cparams(("arbitrary", "arbitrary")), name=name)(
        do, qkv, qkv, qkv, z, z, alog_vec, dtb_vec, nw, ssave)


def colsum(a, name, tt=512):
    T, N = a.shape
    tt = min(tt, T)
    tn = _pick(N, 2048, LANES)

    def body(a_ref, o_ref):
        i = pl.program_id(1)
        s = jnp.sum(a_ref[...].astype(f32), axis=0, keepdims=True)

        @pl.when(i == 0)
        def _():
            o_ref[...] = s

        @pl.when(i > 0)
        def _():
            o_ref[...] += s

    return _pcall(body, out_shape=SDS((1, N), f32), grid=(N // tn, T // tt),
                  in_specs=[pl.BlockSpec((tt, tn), lambda j, i: (i, j))], out_specs=pl.BlockSpec((1, tn), lambda j, i: (0, j)),
                  compiler_params=_cparams(("parallel", "arbitrary")), name=name)(a)


def loss_and_grad(y, target, name, tt=256):
    T, D = y.shape
    tt = min(tt, T)

    def body(y_ref, t_ref, loss_ref, dy_ref):
        i = pl.program_id(0)
        e = y_ref[...] - t_ref[...]
        dy_ref[...] = e * (1.0 / D)
        part = jnp.sum(jnp.sum(e * e, axis=1, keepdims=True), axis=0, keepdims=True) * (0.5 / D)
        part = jnp.broadcast_to(part, (1, LANES))

        @pl.when(i == 0)
        def _():
            loss_ref[...] = part

        @pl.when(i > 0)
        def _():
            loss_ref[...] += part

    blk = pl.BlockSpec((tt, D), lambda i: (i, 0))
    return _pcall(body, out_shape=[SDS((1, LANES), f32), SDS((T, D), f32)], grid=(T // tt,), in_specs=[blk, blk],
                  out_specs=[pl.BlockSpec((1, LANES), lambda i: (0, 0)), blk],
                  compiler_params=_cparams(("arbitrary",)), name=name)(y, target)


def adamw(w, g, m, v, name):
    L, R, C = w.shape
    tr = R if R % 8 else _pick(R, max(8, (1 << 19) // C // 8 * 8), 8)
    c1 = 1.0 / (1.0 - ADAM_B1 ** ADAM_STEP)
    c2 = 1.0 / (1.0 - ADAM_B2 ** ADAM_STEP)

    def body(w_ref, g_ref, m_ref, v_ref, d_ref, nm_ref, nv_ref):
        gg = g_ref[...]
        nm = ADAM_B1 * m_ref[...] + (1.0 - ADAM_B1) * gg
        nv = ADAM_B2 * v_ref[...] + (1.0 - ADAM_B2) * (gg * gg)
        m_hat = nm * c1
        v_hat = nv * c2
        d_ref[...] = -ADAM_LR * (m_hat / (jnp.sqrt(v_hat) + ADAM_EPS) + ADAM_WD * w_ref[...])
        nm_ref[...] = nm
        nv_ref[...] = nv

    blk = pl.BlockSpec((1, tr, C), lambda l, r: (l, r, 0))
    shp = SDS((L, R, C), f32)
    return _pcall(body, out_shape=[shp, shp, shp], grid=(L, R // tr), in_specs=[blk] * 4, out_specs=[blk] * 3,
                  compiler_params=_cparams(("parallel", "parallel")), name=name)(w, g, m, v)


def slot_sum(a, out_dtype, name, extra=None):
    S, R, C = a.shape
    tr = _pick(R, max(16, (1 << 18) // C // 16 * 16), 16)
    n_in = 1 if extra is None else 2

    def body(*refs):
        o_ref = refs[n_in]
        if extra is None:
            acc = refs[0][0].astype(f32)
            for s in range(1, S):
                acc = acc + refs[0][s].astype(f32)
            o_ref[...] = acc.astype(out_dtype)
        else:
            o_ref[...] = (refs[0][...].astype(f32) + refs[1][...].astype(f32)).astype(out_dtype)

    blk = pl.BlockSpec((S, tr, C), lambda r: (0, r, 0))
    if extra is None:
        return _pcall(body, out_shape=SDS((R, C), out_dtype), grid=(R // tr,), in_specs=[blk],
                      out_specs=pl.BlockSpec((tr, C), lambda r: (r, 0)), compiler_params=_cparams(("parallel",)), name=name)(a)
    return _pcall(body, out_shape=SDS((S, R, C), out_dtype), grid=(R // tr,), in_specs=[blk, blk], out_specs=blk,
                  compiler_params=_cparams(("parallel",)), name=name)(a, extra)


HBM_SPEC = pl.BlockSpec(memory_space=pltpu.HBM)


def all_gather(shard, name):
    R, C = shard.shape

    def body(x_ref, out_ref, send_sems, recv_sems, local_sem):
        x, y, c = lax.axis_index("x"), lax.axis_index("y"), lax.axis_index("c")
        me, sibling = (x, y, c), (x, y, 1 - c)
        chips = [(1 - x, y), (x, 1 - y), (1 - x, 1 - y)]

        def slot(px, py, pc):
            return out_ref.at[4 * px + 2 * py + pc]

        def copy(k, block, to, src=None):
            return pltpu.make_async_remote_copy(src_ref=slot(*block) if src is None else src, dst_ref=slot(*block),
                                                send_sem=send_sems.at[k], recv_sem=recv_sems.at[k],
                                                device_id=to, device_id_type=MESH)

        mine = pltpu.make_async_copy(x_ref, slot(*me), local_sem)
        mine.start()
        first = [copy(0, me, sibling, src=x_ref)]
        first += [copy(1 + j, me, (*chip, c), src=x_ref) for j, chip in enumerate(chips)]
        for cp in first:
            cp.start()
        passed = [copy(4 + j, (*chip, c), sibling) for j, chip in enumerate(chips)]
        for j, chip in enumerate(chips):
            copy(1 + j, (*chip, c), me).wait_recv()
            passed[j].start()
        copy(0, sibling, me).wait_recv()
        for j, chip in enumerate(chips):
            copy(4 + j, (*chip, 1 - c), me).wait_recv()
        for cp in first + passed:
            cp.wait_send()
        mine.wait()

    return _pcall(body, out_shape=SDS((N_DEV, R, C), shard.dtype), in_specs=[HBM_SPEC], out_specs=HBM_SPEC,
                  scratch_shapes=[pltpu.SemaphoreType.DMA((7,)), pltpu.SemaphoreType.DMA((7,)), pltpu.SemaphoreType.DMA(())],
                  name=name)(shard)


def exchange_sibling(send, name):
    def body(s_ref, r_ref, send_sem, recv_sem):
        x, y, c = lax.axis_index("x"), lax.axis_index("y"), lax.axis_index("c")
        cp = pltpu.make_async_remote_copy(src_ref=s_ref, dst_ref=r_ref, send_sem=send_sem, recv_sem=recv_sem,
                                          device_id=(x, y, 1 - c), device_id_type=MESH)
        cp.start()
        cp.wait()

    return _pcall(body, out_shape=SDS(send.shape, send.dtype), in_specs=[HBM_SPEC], out_specs=HBM_SPEC,
                  scratch_shapes=[pltpu.SemaphoreType.DMA(()), pltpu.SemaphoreType.DMA(())], name=name)(send)


def exchange_chips(s1, name):
    def body(s_ref, r_ref, send_sems, recv_sems, local_sem):
        x, y, c = lax.axis_index("x"), lax.axis_index("y"), lax.axis_index("c")
        my_chip = 2 * x + y
        chips = [(1 - x, y), (x, 1 - y), (1 - x, 1 - y)]
        mine = pltpu.make_async_copy(s_ref.at[my_chip], r_ref.at[my_chip], local_sem)
        mine.start()
        cps = []
        for j, (px, py) in enumerate(chips):
            cps.append(pltpu.make_async_remote_copy(src_ref=s_ref.at[2 * px + py], dst_ref=r_ref.at[my_chip],
                                                    send_sem=send_sems.at[j], recv_sem=recv_sems.at[j],
                                                    device_id=(px, py, c), device_id_type=MESH))
        for cp in cps:
            cp.start()
        for j, (px, py) in enumerate(chips):
            pltpu.make_async_remote_copy(src_ref=s_ref.at[my_chip], dst_ref=r_ref.at[2 * px + py],
                                         send_sem=send_sems.at[j], recv_sem=recv_sems.at[j],
                                         device_id=(px, py, c), device_id_type=MESH).wait_recv()
        for cp in cps:
            cp.wait_send()
        mine.wait()

    return _pcall(body, out_shape=SDS(s1.shape, s1.dtype), in_specs=[HBM_SPEC], out_specs=HBM_SPEC,
                  scratch_shapes=[pltpu.SemaphoreType.DMA((3,)), pltpu.SemaphoreType.DMA((3,)), pltpu.SemaphoreType.DMA(())],
                  name=name)(s1)


def _to_pack(flat, dtype):
    n = flat.shape[-1]
    unit = 16 * PACK_COLS
    padded = -(-n // unit) * unit
    flat = jnp.pad(flat.astype(dtype), [(0, 0)] * (flat.ndim - 1) + [(0, padded - n)])
    return flat.reshape(flat.shape[:-1] + (padded // PACK_COLS, PACK_COLS))


def reduce_scatter(parts, name):
    n = parts.shape[1]
    c = lax.axis_index("c")
    p = _to_pack(parts, bf16)
    p = p.reshape((4, 2) + p.shape[1:])
    keep = lax.dynamic_index_in_dim(p, c, axis=1, keepdims=False)
    send = lax.dynamic_index_in_dim(p, 1 - c, axis=1, keepdims=False)
    got = exchange_sibling(send, name + "_c")
    s1 = slot_sum(keep, bf16, name + "_add", extra=got)
    got2 = exchange_chips(s1, name + "_xy")
    tot = slot_sum(got2, f32, name + "_sum")
    return tot.reshape(-1)[:n]


BIG = ("w_in", "w_conv_proj", "w_gdn_proj", "w_out", "w_ffn_in", "w_ffn_out")
COL_SHARDED = ("w_in", "w_ffn_in", "conv_dw_w", "short_conv_w")
SMALL = ("b_in", "conv_dw_b", "conv_ln_g", "conv_ln_b", "b_conv_proj", "a_log", "dt_bias", "gdn_norm_w",
         "ln1_g", "ln1_b", "ln2_g", "ln2_b")
CONVW = ("conv_dw_w", "short_conv_w")
ORDER = ("w_in", "b_in", "conv_dw_w", "conv_dw_b", "conv_ln_g", "conv_ln_b", "w_conv_proj", "b_conv_proj",
         "short_conv_w", "a_log", "dt_bias", "gdn_norm_w", "w_gdn_proj", "w_out", "ln1_g", "ln1_b",
         "w_ffn_in", "w_ffn_out", "ln2_g", "ln2_b")


def _full_from_gathered(g, name):
    if name in COL_SHARDED:
        return jnp.moveaxis(g, 0, 1).reshape(g.shape[1], N_DEV * g.shape[2])
    return g.reshape(N_DEV * g.shape[1], g.shape[2])


def _blocks_of_full(full, name):
    if name in COL_SHARDED:
        K, N = full.shape
        return jnp.moveaxis(full.reshape(K, N_DEV, N // N_DEV), 1, 0).reshape(N_DEV, -1)
    return full.reshape(N_DEV, -1)


def _w_in_perm(w, D, H):
    pad = jnp.zeros(w.shape[:-1] + (2 * LANES - 2 * H,), w.dtype)
    return jnp.concatenate([w[..., :6 * D], w[..., 6 * D + 2 * H:], w[..., 6 * D:6 * D + 2 * H], pad], axis=-1)


def _w_in_unperm(w, D, H):
    return jnp.concatenate([w[..., :6 * D], w[..., 8 * D:8 * D + 2 * H], w[..., 6 * D:8 * D]], axis=-1)


def kernel(x, w_in, b_in, conv_dw_w, conv_dw_b, conv_ln_g, conv_ln_b, w_conv_proj, b_conv_proj, short_conv_w, a_log, dt_bias, gdn_norm_w, w_gdn_proj, w_out, ln1_g, ln1_b, w_ffn_in, w_ffn_out, ln2_g, ln2_b, loss_target, m_w_in, m_b_in, m_conv_dw_w, m_conv_dw_b, m_conv_ln_g, m_conv_ln_b, m_w_conv_proj, m_b_conv_proj, m_short_conv_w, m_a_log, m_dt_bias, m_gdn_norm_w, m_w_gdn_proj, m_w_out, m_ln1_g, m_ln1_b, m_w_ffn_in, m_w_ffn_out, m_ln2_g, m_ln2_b, v_w_in, v_b_in, v_conv_dw_w, v_conv_dw_b, v_conv_ln_g, v_conv_ln_b, v_w_conv_proj, v_b_conv_proj, v_short_conv_w, v_a_log, v_dt_bias, v_gdn_norm_w, v_w_gdn_proj, v_w_out, v_ln1_g, v_ln1_b, v_w_ffn_in, v_w_ffn_out, v_ln2_g, v_ln2_b):
    W = dict(w_in=w_in, b_in=b_in, conv_dw_w=conv_dw_w, conv_dw_b=conv_dw_b, conv_ln_g=conv_ln_g, conv_ln_b=conv_ln_b,
             w_conv_proj=w_conv_proj, b_conv_proj=b_conv_proj, short_conv_w=short_conv_w, a_log=a_log, dt_bias=dt_bias,
             gdn_norm_w=gdn_norm_w, w_gdn_proj=w_gdn_proj, w_out=w_out, ln1_g=ln1_g, ln1_b=ln1_b, w_ffn_in=w_ffn_in,
             w_ffn_out=w_ffn_out, ln2_g=ln2_g, ln2_b=ln2_b)
    MO = dict(w_in=m_w_in, b_in=m_b_in, conv_dw_w=m_conv_dw_w, conv_dw_b=m_conv_dw_b, conv_ln_g=m_conv_ln_g,
              conv_ln_b=m_conv_ln_b, w_conv_proj=m_w_conv_proj, b_conv_proj=m_b_conv_proj, short_conv_w=m_short_conv_w,
              a_log=m_a_log, dt_bias=m_dt_bias, gdn_norm_w=m_gdn_norm_w, w_gdn_proj=m_w_gdn_proj, w_out=m_w_out,
              ln1_g=m_ln1_g, ln1_b=m_ln1_b, w_ffn_in=m_w_ffn_in, w_ffn_out=m_w_ffn_out, ln2_g=m_ln2_g, ln2_b=m_ln2_b)
    VO = dict(w_in=v_w_in, b_in=v_b_in, conv_dw_w=v_conv_dw_w, conv_dw_b=v_conv_dw_b, conv_ln_g=v_conv_ln_g,
              conv_ln_b=v_conv_ln_b, w_conv_proj=v_w_conv_proj, b_conv_proj=v_b_conv_proj, short_conv_w=v_short_conv_w,
              a_log=v_a_log, dt_bias=v_dt_bias, gdn_norm_w=v_gdn_norm_w, w_gdn_proj=v_w_gdn_proj, w_out=v_w_out,
              ln1_g=v_ln1_g, ln1_b=v_ln1_b, w_ffn_in=v_w_ffn_in, w_ffn_out=v_w_ffn_out, ln2_g=v_ln2_g, ln2_b=v_ln2_b)

    _, T, D = x.shape
    DEPTH = w_in.shape[0]
    H = D // HEAD_DIM
    F = w_ffn_out.shape[1] * N_DEV
    alpha = (2.0 * DEPTH) ** 0.25
    lnres_fn = make_lnres_fn(alpha)
    NA = 8 * D + 2 * LANES
    TT = min(256, T)
    dev = 4 * lax.axis_index("x") + 2 * lax.axis_index("y") + lax.axis_index("c")

    cw_sizes = [(n, W[n].shape[1:]) for n in CONVW]
    cw_flat = jnp.concatenate([W[n].reshape(-1) for n in CONVW])
    cw_g = all_gather(_to_pack(cw_flat, f32), "ag_convw").reshape(N_DEV, -1)
    conv_full, off = {}, 0
    for n in CONVW:
        sz = W[n].size
        blk = cw_g[:, off:off + sz].reshape((N_DEV,) + W[n].shape)
        conv_full[n] = jnp.moveaxis(blk, 0, 2).reshape(W[n].shape[0], W[n].shape[1], -1)
        off += sz

    def gather_layer(l):
        flat = jnp.concatenate([W[n][l].reshape(-1) for n in BIG])
        g = all_gather(_to_pack(flat, bf16), f"ag_w{l}").reshape(N_DEV, -1)
        out, off = {}, 0
        for n in BIG:
            shp = W[n].shape[1:]
            sz = shp[0] * shp[1]
            out[n] = _full_from_gathered(g[:, off:off + sz].reshape((N_DEV,) + shp), n)
            off += sz
        out["w_in"] = _w_in_perm(out["w_in"], D, H)
        return out

    def lane_vec(v, off=0):
        return jnp.pad(v, (off, LANES - off - v.shape[0])).reshape(1, LANES)

    def row(v):
        return v.reshape(1, -1)

    h32 = x.reshape(T, D)
    h16 = h32.astype(bf16)
    saved = []
    for l in range(DEPTH):
        G = gather_layer(l)
        b_all = _w_in_perm(b_in[l], D, H)
        alog_vec, dtb_vec = lane_vec(a_log[l], H), lane_vec(dt_bias[l], H)
        nw = row(gdn_norm_w[l])
        z = matmul(h16, G["w_in"], "nn", f"l{l}_mm_in", bias=b_all)
        c0, = rowwise(glu_fn, [(z, 0, D), (z, 1, D)], [], [(f32,)], TT, f"l{l}_glu")
        c1 = conv_fwd(c0, 0, conv_full["conv_dw_w"][l], 0, D, CONV_WIDTH, row(conv_dw_b[l]), f"l{l}_conv")
        c3, = rowwise(lnsilu_fn, [(c1, 0, D)], [row(conv_ln_g[l]), row(conv_ln_b[l])], [(bf16,)], TT, f"l{l}_lnsilu")
        yc = matmul(c3, G["w_conv_proj"], "nn", f"l{l}_mm_cp", bias=b_conv_proj[l])
        qkv = conv_fwd(z, 2 * D, conv_full["short_conv_w"][l], 0, 3 * D, SHORT_CONV, None, f"l{l}_sconv")
        og, ssave = gdn_fwd(qkv, z, 5 * D, 8 * D, alog_vec, dtb_vec, nw, f"l{l}_gdn")
        yg = matmul(og, G["w_gdn_proj"], "nn", f"l{l}_mm_gp")
        m, = rowwise(merge_fn, [(z, 6, D), (z, 7, D), (yc, 0, D), (yg, 0, D)], [], [(bf16,)], TT, f"l{l}_merge")
        mix = matmul(m, G["w_out"], "nn", f"l{l}_mm_out")
        x1_32, x1_16 = rowwise(lnres_fn, [(h32, 0, D), (mix, 0, D)], [row(ln1_g[l]), row(ln1_b[l])], [(f32, bf16)], TT, f"l{l}_ln1")
        hf = matmul(x1_16, G["w_ffn_in"], "nn", f"l{l}_mm_fi")
        act, = rowwise(swiglu_fn, [(hf, 0, F), (hf, 1, F)], [], [(bf16,)], min(128, T), f"l{l}_swiglu")
        ff = matmul(act, G["w_ffn_out"], "nn", f"l{l}_mm_fo")
        x2_32, x2_16 = rowwise(lnres_fn, [(x1_32, 0, D), (ff, 0, D)], [row(ln2_g[l]), row(ln2_b[l])], [(f32, bf16)], TT, f"l{l}_ln2")
        saved.append(dict(G=G, h32=h32, h16=h16, z=z, c0=c0, c1=c1, c3=c3, yc=yc, qkv=qkv, ssave=ssave, og=og, yg=yg, m=m,
                          mix=mix, x1_32=x1_32, x1_16=x1_16, hf=hf, act=act, ff=ff, alog_vec=alog_vec, dtb_vec=dtb_vec, nw=nw))
        h32, h16 = x2_32, x2_16

    loss_vec, dy = loss_and_grad(h32, loss_target.reshape(T, D), "loss")
    loss = lax.psum(loss_vec[0, 0], AXES)

    gsmall = {n: [None] * DEPTH for n in SMALL + CONVW}
    gbig = {n: [None] * DEPTH for n in BIG}
    dh = [dy]
    for l in reversed(range(DEPTH)):
        s = saved[l]
        G = s["G"]
        TB = min(128, T)
        d_x1r, d_ff, dg2, db2 = rowwise_vjp(lnres_fn, [(s["x1_32"], 0, D), (s["ff"], 0, D)], [row(ln2_g[l]), row(ln2_b[l])],
                                            [dh], [f32, bf16], TB, f"l{l}_ln2_b")
        d_act = matmul(d_ff, G["w_ffn_out"], "nt", f"l{l}_mm_fo_dx")
        dw_fo = matmul(s["act"], d_ff, "tn", f"l{l}_mm_fo_dw", tk_cap=1024)
        d_gate, d_up = rowwise_vjp(swiglu_fn, [(s["hf"], 0, F), (s["hf"], 1, F)], [], [[d_act]], [bf16, bf16], min(64, T), f"l{l}_swiglu_b")
        d_hf = jnp.concatenate([d_gate, d_up], axis=1)
        d_x1m = matmul(d_hf, G["w_ffn_in"], "nt", f"l{l}_mm_fi_dx")
        dw_fi = matmul(s["x1_16"], d_hf, "tn", f"l{l}_mm_fi_dw", tk_cap=1024)
        d_hr, d_mix, dg1, db1 = rowwise_vjp(lnres_fn, [(s["h32"], 0, D), (s["mix"], 0, D)], [row(ln1_g[l]), row(ln1_b[l])],
                                            [[d_x1r, d_x1m]], [f32, bf16], TB, f"l{l}_ln1_b")
        d_m = matmul(d_mix, G["w_out"], "nt", f"l{l}_mm_out_dx")
        dw_out = matmul(s["m"], d_mix, "tn", f"l{l}_mm_out_dw", tk_cap=1024)
        d_ga, d_gb, d_yc, d_yg = rowwise_vjp(merge_fn, [(s["z"], 6, D), (s["z"], 7, D), (s["yc"], 0, D), (s["yg"], 0, D)], [],
                                             [[d_m]], [bf16] * 4, TB, f"l{l}_merge_b")
        d_c3 = matmul(d_yc, G["w_conv_proj"], "nt", f"l{l}_mm_cp_dx")
        dw_cp = matmul(s["c3"], d_yc, "tn", f"l{l}_mm_cp_dw", tk_cap=1024)
        db_cp = colsum(d_yc, f"l{l}_cs_cp")
        d_c1, dcg, dcb = rowwise_vjp(lnsilu_fn, [(s["c1"], 0, D)], [row(conv_ln_g[l]), row(conv_ln_b[l])], [[d_c3]], [f32], TB, f"l{l}_lnsilu_b")
        d_c0, dw31, db31 = conv_bwd(d_c1, s["c0"], 0, conv_full["conv_dw_w"][l], 0, D, CONV_WIDTH, f32, f"l{l}_conv_b")
        d_glu_a, d_glu_b = rowwise_vjp(glu_fn, [(s["z"], 0, D), (s["z"], 1, D)], [], [[d_c0]], [bf16, bf16], TB, f"l{l}_glu_b")
        d_og = matmul(d_yg, G["w_gdn_proj"], "nt", f"l{l}_mm_gp_dx")
        dw_gp = matmul(s["og"], d_yg, "tn", f"l{l}_mm_gp_dw", tk_cap=1024)
        dq, dk, dv, d_zg, d_zs, dalog, ddtb, dnw = gdn_bwd(d_og, s["qkv"], s["z"], 5 * D, 8 * D, s["alog_vec"], s["dtb_vec"], s["nw"],
                                                            s["ssave"], f"l{l}_gdn_b")
        dxs, dwss = [], []
        for sec, dsec in enumerate((dq, dk, dv)):
            dxp, dwp, _ = conv_bwd(dsec, s["z"], (2 + sec) * D, conv_full["short_conv_w"][l], sec * D, D, SHORT_CONV, bf16,
                                   f"l{l}_sconv_b{sec}")
            dxs.append(dxp)
            dwss.append(dwp)
        dz = jnp.concatenate([d_glu_a, d_glu_b] + dxs + [d_zg, d_ga, d_gb, d_zs.astype(bf16), jnp.zeros((T, LANES), bf16)], axis=1)
        d_hm = matmul(dz, G["w_in"], "nt", f"l{l}_mm_in_dx")
        dw_in = matmul(s["h16"], dz, "tn", f"l{l}_mm_in_dw", tk_cap=1024)
        db_all = colsum(dz, f"l{l}_cs_in")
        dh = [d_hr, d_hm]

        gsmall["b_in"][l] = _w_in_unperm(db_all[0], D, H)
        gsmall["conv_dw_b"][l] = db31[0]
        gsmall["conv_ln_g"][l], gsmall["conv_ln_b"][l] = dcg[0], dcb[0]
        gsmall["b_conv_proj"][l] = db_cp[0]
        gsmall["a_log"][l], gsmall["dt_bias"][l] = dalog[0, H:2 * H], ddtb[0, H:2 * H]
        gsmall["gdn_norm_w"][l] = dnw[0]
        gsmall["ln1_g"][l], gsmall["ln1_b"][l], gsmall["ln2_g"][l], gsmall["ln2_b"][l] = dg1[0], db1[0], dg2[0], db2[0]
        gsmall["conv_dw_w"][l] = dw31
        gsmall["short_conv_w"][l] = jnp.concatenate(dwss, axis=1)

        full = dict(w_in=_w_in_unperm(dw_in, D, H), w_conv_proj=dw_cp, w_gdn_proj=dw_gp, w_out=dw_out, w_ffn_in=dw_fi, w_ffn_out=dw_fo)
        parts = jnp.concatenate([_blocks_of_full(full[n], n) for n in BIG], axis=1)
        red = reduce_scatter(parts, f"rs{l}")
        off = 0
        for n in BIG:
            shp = W[n].shape[1:]
            sz = shp[0] * shp[1]
            gbig[n][l] = red[off:off + sz].reshape(shp)
            off += sz

    grad_x, = rowwise(add_fn, [(dh[0], 0, D), (dh[1], 0, D)], [], [(f32,)], TT, "grad_x")
    grad_x = grad_x.reshape(1, T, D)

    small_flat = jnp.concatenate([jnp.stack(gsmall[n]).reshape(-1) for n in SMALL + CONVW])
    sg = all_gather(_to_pack(small_flat, f32), "ag_small")
    small_tot = slot_sum(sg, f32, "small_sum").reshape(-1)
    grads, off = {}, 0
    for n in SMALL:
        grads[n] = small_tot[off:off + W[n].size].reshape(W[n].shape)
        off += W[n].size
    for n in CONVW:
        L_, K_, c_ = W[n].shape
        fullg = small_tot[off:off + L_ * K_ * c_ * N_DEV].reshape(L_, K_, c_ * N_DEV)
        grads[n] = lax.dynamic_slice_in_dim(fullg, dev * c_, c_, axis=2)
        off += L_ * K_ * c_ * N_DEV
    for n in BIG:
        grads[n] = jnp.stack(gbig[n])

    delta, new_m, new_v = {}, {}, {}
    for n in BIG + CONVW:
        delta[n], new_m[n], new_v[n] = adamw(W[n], grads[n], MO[n], VO[n], f"adamw_{n}")
    pk = lambda d: _to_pack(jnp.concatenate([d[n].reshape(-1) for n in SMALL]), f32)[None]
    ds, ms, vs = adamw(pk(W), pk(grads), pk(MO), pk(VO), "adamw_small")
    off = 0
    for n in SMALL:
        sl = lambda a: a.reshape(-1)[off:off + W[n].size].reshape(W[n].shape)
        delta[n], new_m[n], new_v[n] = sl(ds), sl(ms), sl(vs)
        off += W[n].size

    return (loss, grad_x, *[grads[n] for n in ORDER], *[delta[n] for n in ORDER],
            *[new_m[n] for n in ORDER], *[new_v[n] for n in ORDER])
```

```python
import functools
import math

import jax
import jax.numpy as jnp
from jax import lax
from jax.experimental import pallas as pl
from jax.experimental.pallas import tpu as pltpu

f32, bf16 = jnp.float32, jnp.bfloat16
SDS = jax.ShapeDtypeStruct
MESH = pl.DeviceIdType.MESH
AXES = ("x", "y", "c")
N_DEV = 8

CONV_WIDTH = 31
SHORT_CONV = 4
HEAD_DIM = 128
CHUNK = 64
LN_EPS = 1e-5
ADAM_LR, ADAM_B1, ADAM_B2, ADAM_EPS, ADAM_WD, ADAM_STEP = 0.001, 0.9, 0.999, 1e-08, 0.01, 10

LANES = 128
HALO = 32
PACK_COLS = 1024
VMEM_LIMIT = 56 * 1024 * 1024


def _pcall(body, **kw):
    return pl.pallas_call(body, **kw)


def _cparams(sem=None):
    return pltpu.CompilerParams(dimension_semantics=sem, vmem_limit_bytes=VMEM_LIMIT)


def _pick(n, cap, mult):
    if n <= cap:
        return n
    for t in range(cap - cap % mult, 0, -mult):
        if n % t == 0:
            return t
    raise ValueError(f"no tile for {n} under {cap} in steps of {mult}")


def matmul(a, b, mode, name, bias=None, out_dtype=f32, tm_cap=1024, tn_cap=1280, tk_cap=2048,
           b_blocked=False, out_blocked=False):
    if b_blocked:
        nb, br, bc = b.shape
        b2 = (br, nb * bc)
    else:
        b2 = b.shape
    if mode == "nn":
        (M, K), (K2, N) = a.shape, b2
    elif mode == "nt":
        (M, K), (N, K2) = a.shape, b2
    else:
        (K, M), (K2, N) = a.shape, b2
    assert K == K2, (a.shape, b.shape, mode)
    tm = _pick(M, tm_cap, 256 if M % 256 == 0 else 8)
    tn = _pick(N, tn_cap, 256 if N % 256 == 0 else LANES)
    tk = _pick(K, tk_cap, 256 if K % 256 == 0 else LANES)
    if b_blocked and mode == "nn":
        tn = N // N_DEV
    if b_blocked and mode == "nt":
        tk = K // N_DEV
    if out_blocked:
        tn = N // N_DEV
    nm, nn, nk = M // tm, N // tn, K // tk
    dims = {"nn": (((1,), (0,)), ((), ())), "nt": (((1,), (1,)), ((), ())), "tn": (((0,), (0,)), ((), ()))}[mode]
    has_bias = bias is not None

    def body(*refs):
        a_ref, b_ref = refs[0], refs[1]
        bias_ref = refs[2] if has_bias else None
        o_ref = refs[2 + has_bias]
        prod = lax.dot_general(a_ref[...], b_ref[...], dims, preferred_element_type=f32)

        def finish(acc):
            if has_bias:
                acc = acc + bias_ref[...]
            o_ref[...] = acc.astype(out_dtype)

        if nk == 1:
            finish(prod)
        else:
            acc_ref = refs[3 + has_bias]
            k = pl.program_id(2)

            @pl.when(k == 0)
            def _():
                acc_ref[...] = prod

            @pl.when(jnp.logical_and(k > 0, k < nk - 1))
            def _():
                acc_ref[...] += prod

            @pl.when(k == nk - 1)
            def _():
                finish(acc_ref[...] + prod)

    a_bytes, b_bytes = M * K, N * K
    m_outer = a_bytes >= b_bytes
    if m_outer:
        grid = (nm, nn, nk)
        gi = lambda i, j, k: (i, j, k)
    else:
        grid = (nn, nm, nk)
        gi = lambda j, i, k: (i, j, k)

    def amap(*g):
        i, j, k = gi(*g)
        return (k, i) if mode == "tn" else (i, k)

    def bmap(*g):
        i, j, k = gi(*g)
        return (j, k) if mode == "nt" else (k, j)

    def omap(*g):
        i, j, k = gi(*g)
        return (i, j)

    def biasmap(*g):
        i, j, k = gi(*g)
        return (0, j)

    def bmap_blocked(*g):
        i, j, k = gi(*g)
        return (k, j, 0) if mode == "nt" else (j, k, 0)

    def omap_blocked(*g):
        i, j, k = gi(*g)
        return (j, i, 0)

    b_block = (tn, tk) if mode == "nt" else (tk, tn)
    in_specs = [pl.BlockSpec((tk, tm) if mode == "tn" else (tm, tk), amap),
                pl.BlockSpec((None,) + b_block, bmap_blocked) if b_blocked else pl.BlockSpec(b_block, bmap)]
    args = [a, b]
    if has_bias:
        in_specs.append(pl.BlockSpec((1, tn), biasmap))
        args.append(bias.reshape(1, N).astype(f32))
    if out_blocked:
        out_shape, out_spec = SDS((N_DEV, M, tn), out_dtype), pl.BlockSpec((None, tm, tn), omap_blocked)
    else:
        out_shape, out_spec = SDS((M, N), out_dtype), pl.BlockSpec((tm, tn), omap)
    return _pcall(body, out_shape=out_shape, grid=grid, in_specs=in_specs,
                  out_specs=out_spec,
                  scratch_shapes=[pltpu.VMEM((tm, tn), f32)] if nk > 1 else [],
                  compiler_params=_cparams(("parallel", "parallel", "arbitrary")), name=name)(*args)


def _row_specs(rows, tt):
    specs, args = [], []
    for arr, cb, width in rows:
        specs.append(pl.BlockSpec((tt, width), lambda i, cb=cb: (i, cb)))
        args.append(arr)
    return specs, args


def _param_specs(params):
    return [pl.BlockSpec(p.shape, lambda i: (0, 0)) for p in params]


def rowwise(fn, rows, params, out_dtypes, tt, name):
    T = rows[0][0].shape[0]
    nr, npar = len(rows), len(params)
    blocks = [SDS((tt, w), f32) for _, _, w in rows] + [SDS(p.shape, f32) for p in params]
    outs = jax.eval_shape(fn, *blocks)
    out_shape, out_specs = [], []
    for o, dts in zip(outs, out_dtypes):
        for dt in dts:
            out_shape.append(SDS((T, o.shape[1]), dt))
            out_specs.append(pl.BlockSpec((tt, o.shape[1]), lambda i: (i, 0)))

    def body(*refs):
        xs = [r[...].astype(f32) for r in refs[:nr + npar]]
        res = fn(*xs)
        k = nr + npar
        for o, dts in zip(res, out_dtypes):
            for dt in dts:
                refs[k][...] = o.astype(dt)
                k += 1

    rspecs, rargs = _row_specs(rows, tt)
    return _pcall(body, out_shape=out_shape, grid=(T // tt,), in_specs=rspecs + _param_specs(params),
                  out_specs=out_specs, compiler_params=_cparams(("parallel",)), name=name)(*rargs, *params)


def rowwise_vjp(fn, rows, params, cots, d_dtypes, tt, name):
    T = rows[0][0].shape[0]
    nr, npar = len(rows), len(params)
    ncot = [len(c) for c in cots]
    flat_cots = [c for cs in cots for c in cs]

    def body(*refs):
        i = pl.program_id(0)
        xs = [r[...].astype(f32) for r in refs[:nr + npar]]
        k = nr + npar
        cs = []
        for n in ncot:
            tot = refs[k][...].astype(f32)
            for r in refs[k + 1:k + n]:
                tot = tot + r[...].astype(f32)
            cs.append(tot)
            k += n
        _, pull = jax.vjp(fn, *xs)
        grads = pull(tuple(cs))
        for g, dt in zip(grads[:nr], d_dtypes):
            refs[k][...] = g.astype(dt)
            k += 1
        for g in grads[nr:]:
            ref = refs[k]
            k += 1

            @pl.when(i == 0)
            def _(ref=ref, g=g):
                ref[...] = g

            @pl.when(i > 0)
            def _(ref=ref, g=g):
                ref[...] += g

    rspecs, rargs = _row_specs(rows, tt)
    cot_specs = [pl.BlockSpec((tt, c.shape[1]), lambda i: (i, 0)) for c in flat_cots]
    out_shape = [SDS((T, w), dt) for (_, _, w), dt in zip(rows, d_dtypes)] + [SDS(p.shape, f32) for p in params]
    out_specs = [pl.BlockSpec((tt, w), lambda i: (i, 0)) for _, _, w in rows] + _param_specs(params)
    return _pcall(body, out_shape=out_shape, grid=(T // tt,), in_specs=rspecs + _param_specs(params) + cot_specs,
                  out_specs=out_specs, compiler_params=_cparams(("arbitrary",)), name=name)(*rargs, *params, *flat_cots)


def _sigmoid(x):
    return jax.nn.sigmoid(x)


def _silu(x):
    return x * jax.nn.sigmoid(x)


def _softplus(x):
    return jnp.maximum(x, 0.0) + jnp.log(1.0 + jnp.exp(-jnp.abs(x)))


def _layer_norm(x, g, b):
    mu = jnp.mean(x, axis=-1, keepdims=True)
    xc = x - mu
    var = jnp.mean(xc * xc, axis=-1, keepdims=True)
    return xc * lax.rsqrt(var + LN_EPS) * g + b


def glu_fn(a, b):
    return (a * _sigmoid(b),)


def lnsilu_fn(c, g, b):
    return (_silu(_layer_norm(c, g, b)),)


def merge_fn(ga, gb, yc, yg):
    return (_sigmoid(ga) * yc + _sigmoid(gb) * yg,)


def swiglu_fn(gate, up):
    return (_silu(gate) * up,)


def make_lnres_fn(alpha):
    def lnres_fn(h, y, g, b):
        return (_layer_norm(alpha * h + y, g, b),)
    return lnres_fn


def add_fn(a, b):
    return (a + b,)


CONV_ROWS = 64


def _tap_groups(K, first_row):
    groups = {}
    for j in range(K):
        groups.setdefault((first_row + j) % 8, []).append(j)
    out = []
    for taps in groups.values():
        start = first_row + taps[0]
        out.append((start, 8 * (len(taps) - 1), [(j, first_row + j - start) for j in taps]))
    return out


def conv_fwd(x, x_col0, w, w_col0, width, K, bias, name, tt=512, tc=256):
    T = x.shape[0]
    tt = min(tt, T)
    assert width % tc == 0 and x_col0 % tc == 0 and w_col0 % tc == 0 and tt % HALO == 0 and tt % CONV_ROWS == 0
    hb = tt // HALO
    xb, wb = x_col0 // tc, w_col0 // tc
    has_bias = bias is not None
    groups = _tap_groups(K, HALO - (K - 1))

    def body(*refs):
        x_ref, halo_ref, w_ref = refs[:3]
        b_ref = refs[3] if has_bias else None
        o_ref, ext_ref = refs[3 + has_bias], refs[4 + has_bias]
        i = pl.program_id(1)
        ext_ref[pl.ds(0, HALO), :] = jnp.where(i > 0, halo_ref[...], 0.0)
        ext_ref[pl.ds(HALO, tt), :] = x_ref[...]
        for r in range(tt // CONV_ROWS):
            acc = jnp.zeros((CONV_ROWS, tc), f32)
            for start, extra, taps in groups:
                win = ext_ref[pl.ds(r * CONV_ROWS + start, CONV_ROWS + extra), :]
                for j, off in taps:
                    acc = acc + w_ref[j:j + 1, :] * win[off:off + CONV_ROWS]
            if has_bias:
                acc = acc + b_ref[...]
            o_ref[pl.ds(r * CONV_ROWS, CONV_ROWS), :] = acc

    in_specs = [pl.BlockSpec((tt, tc), lambda cb, i: (i, xb + cb)),
                pl.BlockSpec((HALO, tc), lambda cb, i: (jnp.maximum(i * hb - 1, 0), xb + cb)),
                pl.BlockSpec((K, tc), lambda cb, i: (0, wb + cb))]
    args = [x, x, w]
    if has_bias:
        in_specs.append(pl.BlockSpec((1, tc), lambda cb, i: (0, cb)))
        args.append(bias)
    return _pcall(body, out_shape=SDS((T, width), f32), grid=(width // tc, T // tt), in_specs=in_specs,
                  out_specs=pl.BlockSpec((tt, tc), lambda cb, i: (i, cb)),
                  scratch_shapes=[pltpu.VMEM((tt + HALO, tc), f32)],
                  compiler_params=_cparams(("parallel", "arbitrary")), name=name)(*args)


def conv_bwd(dy, x, x_col0, w, w_col0, width, K, dx_dtype, name, tt=512, tc=256):
    T = x.shape[0]
    tt = min(tt, T)
    hb = tt // HALO
    nt = T // tt
    xb, wb = x_col0 // tc, w_col0 // tc
    x_groups = _tap_groups(K, HALO - (K - 1))
    dy_groups = [(start, extra, [(K - 1 - o, off) for o, off in taps]) for start, extra, taps in _tap_groups(K, 0)]
    RC = CONV_ROWS

    def body(dy_ref, dyn_ref, x_ref, halo_ref, w_ref, dx_ref, dw_ref, db_ref, xext_ref, dyext_ref, dwacc_ref, dbacc_ref):
        i = pl.program_id(1)

        @pl.when(i == 0)
        def _():
            dwacc_ref[...] = jnp.zeros_like(dwacc_ref)
            dbacc_ref[...] = jnp.zeros_like(dbacc_ref)

        xext_ref[pl.ds(0, HALO), :] = jnp.where(i > 0, halo_ref[...], 0.0)
        xext_ref[pl.ds(HALO, tt), :] = x_ref[...]
        dyext_ref[pl.ds(0, tt), :] = dy_ref[...].astype(f32)
        dyext_ref[pl.ds(tt, HALO), :] = jnp.where(i < nt - 1, dyn_ref[...].astype(f32), 0.0)

        def fold(p):
            return jnp.sum(p.reshape(RC // 8, 8, tc), axis=0)

        for r in range(tt // RC):
            acc = jnp.zeros((RC, tc), f32)
            for start, extra, taps in dy_groups:
                win = dyext_ref[pl.ds(r * RC + start, RC + extra), :]
                for j, off in taps:
                    acc = acc + w_ref[j:j + 1, :] * win[off:off + RC]
            dx_ref[pl.ds(r * RC, RC), :] = acc.astype(dx_dtype)
            dyc = dyext_ref[pl.ds(r * RC, RC), :]
            dbacc_ref[...] += fold(dyc)
            for start, extra, taps in x_groups:
                win = xext_ref[pl.ds(r * RC + start, RC + extra), :]
                for j, off in taps:
                    dwacc_ref[j] += fold(dyc * win[off:off + RC])

        @pl.when(i == nt - 1)
        def _():
            dw_ref[...] = jnp.sum(dwacc_ref[...], axis=1)
            db_ref[...] = jnp.sum(dbacc_ref[...], axis=0, keepdims=True)

    in_specs = [pl.BlockSpec((tt, tc), lambda cb, i: (i, cb)),
                pl.BlockSpec((HALO, tc), lambda cb, i: (jnp.minimum((i + 1) * hb, nt * hb - 1), cb)),
                pl.BlockSpec((tt, tc), lambda cb, i: (i, xb + cb)),
                pl.BlockSpec((HALO, tc), lambda cb, i: (jnp.maximum(i * hb - 1, 0), xb + cb)),
                pl.BlockSpec((K, tc), lambda cb, i: (0, wb + cb))]
    out_shape = [SDS((T, width), dx_dtype), SDS((K, width), f32), SDS((1, width), f32)]
    out_specs = [pl.BlockSpec((tt, tc), lambda cb, i: (i, cb)), pl.BlockSpec((K, tc), lambda cb, i: (0, cb)),
                 pl.BlockSpec((1, tc), lambda cb, i: (0, cb))]
    return _pcall(body, out_shape=out_shape, grid=(width // tc, nt), in_specs=in_specs, out_specs=out_specs,
                  scratch_shapes=[pltpu.VMEM((tt + HALO, tc), f32), pltpu.VMEM((tt + HALO, tc), f32),
                                  pltpu.VMEM((K, 8, tc), f32), pltpu.VMEM((8, tc), f32)],
                  compiler_params=_cparams(("parallel", "arbitrary")), name=name)(dy, dy, x, x, w)


def _mm_b(eq, a, b):
    return jnp.einsum(eq, a.astype(bf16), b.astype(bf16), preferred_element_type=f32)


def _split_bf16(a):
    hi = a.astype(bf16)
    return hi, (a - hi.astype(f32)).astype(bf16)


def _mm_3(eq, a, b):
    ah, al = _split_bf16(a)
    bh, bl = _split_bf16(b)
    e = lambda x, y: jnp.einsum(eq, x, y, preferred_element_type=f32)
    return e(ah, bh) + (e(ah, bl) + e(al, bh))


def _unit_lower_inverse(L):
    C = L.shape[-1]
    ii = lax.broadcasted_iota(jnp.int32, (C, C), 0)
    jj = lax.broadcasted_iota(jnp.int32, (C, C), 1)
    P = -L
    inv = (ii == jj).astype(f32)[None] + P
    span = 2
    while span < C:
        P = _mm_3("hij,hjk->hik", P, P)
        inv = inv + _mm_3("hij,hjk->hik", inv, P)
        span *= 2
    return inv


@jax.custom_vjp
def _known_inverse(L, inv):
    return inv


def _known_inverse_fwd(L, inv):
    return inv, inv


def _known_inverse_bwd(inv, g):
    t = _mm_3("hji,hjk->hik", inv, g)
    return -_mm_3("hij,hkj->hik", t, inv), jnp.zeros_like(inv)


_known_inverse.defvjp(_known_inverse_fwd, _known_inverse_bwd)


def _gdn_chunk(n_heads, head0, inv_known, S, qr, kr, vr, zg, zs, alog, dtb, nw):
    HB, C, dk = qr.shape
    q = _silu(qr)
    k = _silu(kr)
    v = _silu(vr)
    q = q * lax.rsqrt(jnp.sum(q * q, axis=-1, keepdims=True) + 1e-6) * (dk ** -0.5)
    k = k * lax.rsqrt(jnp.sum(k * k, axis=-1, keepdims=True) + 1e-6)
    beta_all = _sigmoid(zs)
    la_all = -jnp.exp(alog) * _softplus(zs + dtb)
    lane = lax.broadcasted_iota(jnp.int32, (C, LANES), 1)
    betas, las = [], []
    for h in range(HB):
        betas.append(jnp.sum(jnp.where(lane == head0 + h, beta_all, 0.0), axis=-1, keepdims=True))
        las.append(jnp.sum(jnp.where(lane == n_heads + head0 + h, la_all, 0.0), axis=-1, keepdims=True))
    beta = jnp.stack(betas, axis=0)
    la = jnp.stack(las, axis=0)
    ii = lax.broadcasted_iota(jnp.int32, (C, C), 0)
    jj = lax.broadcasted_iota(jnp.int32, (C, C), 1)
    eye = (ii == jj).astype(f32)[None]
    causal = (jj <= ii)[None]
    strict = (jj < ii)[None]
    la_row = jnp.sum(la * eye, axis=1, keepdims=True)
    g_col = jnp.sum(jnp.where(causal, la_row, 0.0), axis=2, keepdims=True)
    g_row = jnp.sum(jnp.where((ii <= jj)[None], la, 0.0), axis=1, keepdims=True)
    g_last = jnp.sum(la, axis=1, keepdims=True)
    decay = jnp.where(causal, jnp.exp(jnp.where(causal, g_col - g_row, 0.0)), 0.0)
    kb = k * beta
    L = jnp.where(strict, _mm_b("hid,hjd->hij", kb, k) * decay, 0.0)
    inv = _unit_lower_inverse(L) if inv_known is None else _known_inverse(L, inv_known)
    eg = jnp.exp(g_col)
    u = _mm_b("hij,hjd->hid", inv, v * beta)
    w = _mm_b("hij,hjd->hid", inv, kb * eg)
    attn = jnp.where(causal, _mm_b("hid,hjd->hij", q, k) * decay, 0.0)
    qd = q * eg
    kd = k * jnp.exp(g_last - g_col)
    v_new = u - _mm_b("hck,hkv->hcv", w, S)
    o = _mm_b("hck,hkv->hcv", qd, S) + _mm_b("hcj,hjv->hcv", attn, v_new)
    S_new = S * jnp.exp(g_last) + _mm_b("hck,hcv->hkv", kd, v_new)
    o = o * lax.rsqrt(jnp.mean(o * o, axis=-1, keepdims=True) + 1e-6) * nw * _silu(zg)
    return S_new, o, inv


def _split_heads(x, HB):
    return jnp.stack([x[:, h * HEAD_DIM:(h + 1) * HEAD_DIM] for h in range(HB)], axis=0)


def _merge_heads(x):
    return jnp.concatenate([x[h] for h in range(x.shape[0])], axis=-1)


def _gdn_dims(T, D):
    H = D // HEAD_DIM
    HB = min(4, H)
    tt = min(512, T)
    return H, HB, tt, tt // CHUNK


def gdn_fwd(qkv, z, zg_col0, zs_col0, alog_vec, dtb_vec, nw, name):
    T, D3 = qkv.shape
    D = D3 // 3
    H, HB, tt, nc = _gdn_dims(T, D)
    W = HB * HEAD_DIM
    nhg = H // HB
    zgb, zsb = zg_col0 // W, zs_col0 // LANES

    def body(q_ref, k_ref, v_ref, zg_ref, zs_ref, alog_ref, dtb_ref, nw_ref, o_ref, ssave_ref, isave_ref, s_ref):
        i, hg = pl.program_id(0), pl.program_id(1)
        hsl = pl.ds(hg * HB, HB)

        @pl.when(i == 0)
        def _():
            s_ref[hsl] = jnp.zeros((HB, HEAD_DIM, HEAD_DIM), f32)

        alog, dtb, nwv = alog_ref[...], dtb_ref[...], nw_ref[...]

        def step(c, carry):
            rows = pl.ds(pl.multiple_of(c * CHUNK, CHUNK), CHUNK)
            S = s_ref[hsl]
            ssave_ref[:, pl.ds(c, 1)] = S[:, None]
            S_new, o, inv = _gdn_chunk(H, hg * HB, None, S, _split_heads(q_ref[rows, :], HB),
                                       _split_heads(k_ref[rows, :], HB), _split_heads(v_ref[rows, :], HB),
                                       _split_heads(zg_ref[rows, :], HB), zs_ref[rows, :], alog, dtb, nwv)
            s_ref[hsl] = S_new
            isave_ref[:, pl.ds(c, 1)] = inv[:, None]
            o_ref[rows, :] = _merge_heads(o).astype(bf16)
            return carry

        lax.fori_loop(0, nc, step, 0)

    col = lambda off: pl.BlockSpec((tt, W), lambda i, hg, off=off: (i, off + hg))
    in_specs = [col(0), col(nhg), col(2 * nhg), col(zgb), pl.BlockSpec((tt, LANES), lambda i, hg: (i, zsb)),
                pl.BlockSpec((1, LANES), lambda i, hg: (0, 0)), pl.BlockSpec((1, LANES), lambda i, hg: (0, 0)),
                pl.BlockSpec((1, LANES), lambda i, hg: (0, 0))]
    out_shape = [SDS((T, D), bf16), SDS((H, T // CHUNK, HEAD_DIM, HEAD_DIM), f32), SDS((H, T // CHUNK, CHUNK, CHUNK), f32)]
    out_specs = [pl.BlockSpec((tt, W), lambda i, hg: (i, hg)),
                 pl.BlockSpec((HB, nc, HEAD_DIM, HEAD_DIM), lambda i, hg: (hg, i, 0, 0)),
                 pl.BlockSpec((HB, nc, CHUNK, CHUNK), lambda i, hg: (hg, i, 0, 0))]
    return _pcall(body, out_shape=out_shape, grid=(T // tt, nhg), in_specs=in_specs, out_specs=out_specs,
                  scratch_shapes=[pltpu.VMEM((H, HEAD_DIM, HEAD_DIM), f32)],
                  compiler_params=_cparams(("arbitrary", "arbitrary")), name=name)(qkv, qkv, qkv, z, z, alog_vec, dtb_vec, nw)


def gdn_bwd(do, qkv, z, zg_col0, zs_col0, alog_vec, dtb_vec, nw, ssave, isave, name):
    T, D3 = qkv.shape
    D = D3 // 3
    H, HB, tt, nc = _gdn_dims(T, D)
    W = HB * HEAD_DIM
    nhg = H // HB
    nt = T // tt
    zgb, zsb = zg_col0 // W, zs_col0 // LANES

    def body(do_ref, q_ref, k_ref, v_ref, zg_ref, zs_ref, alog_ref, dtb_ref, nw_ref, ssave_ref, isave_ref,
             dq_ref, dk_ref, dv_ref, dzg_ref, dzs_ref, dalog_ref, ddtb_ref, dnw_ref, ds_ref):
        i, hg = pl.program_id(0), pl.program_id(1)
        hsl = pl.ds(hg * HB, HB)

        @pl.when(i == 0)
        def _():
            ds_ref[hsl] = jnp.zeros((HB, HEAD_DIM, HEAD_DIM), f32)

        @pl.when(jnp.logical_and(i == 0, hg == 0))
        def _():
            dalog_ref[...] = jnp.zeros_like(dalog_ref)
            ddtb_ref[...] = jnp.zeros_like(ddtb_ref)
            dnw_ref[...] = jnp.zeros_like(dnw_ref)

        @pl.when(hg == 0)
        def _():
            dzs_ref[...] = jnp.zeros_like(dzs_ref)

        alog, dtb, nwv = alog_ref[...], dtb_ref[...], nw_ref[...]

        def step(cc, carry):
            c = nc - 1 - cc
            rows = pl.ds(pl.multiple_of(c * CHUNK, CHUNK), CHUNK)
            S = ssave_ref[:, pl.ds(c, 1)][:, 0]
            inv = isave_ref[:, pl.ds(c, 1)][:, 0]

            def fn(*xs):
                S_new, o, _ = _gdn_chunk(H, hg * HB, inv, *xs)
                return S_new, o

            _, pull = jax.vjp(fn, S, _split_heads(q_ref[rows, :], HB), _split_heads(k_ref[rows, :], HB),
                              _split_heads(v_ref[rows, :], HB), _split_heads(zg_ref[rows, :], HB),
                              zs_ref[rows, :], alog, dtb, nwv)
            dS, dq, dk, dv, dzg, dzs, dal, ddt, dnw = pull((ds_ref[hsl], _split_heads(do_ref[rows, :], HB)))
            ds_ref[hsl] = dS
            dq_ref[rows, :] = _merge_heads(dq)
            dk_ref[rows, :] = _merge_heads(dk)
            dv_ref[rows, :] = _merge_heads(dv)
            dzg_ref[rows, :] = _merge_heads(dzg).astype(bf16)
            dzs_ref[rows, :] += dzs
            dalog_ref[...] += dal
            ddtb_ref[...] += ddt
            dnw_ref[...] += dnw
            return carry

        lax.fori_loop(0, nc, step, 0)

    rev = lambda i: nt - 1 - i
    col = lambda off: pl.BlockSpec((tt, W), lambda i, hg, off=off: (rev(i), off + hg))
    vec = lambda r: pl.BlockSpec((r, LANES), lambda i, hg: (0, 0))
    in_specs = [col(0), col(0), col(nhg), col(2 * nhg), col(zgb), pl.BlockSpec((tt, LANES), lambda i, hg: (rev(i), zsb)),
                vec(1), vec(1), vec(1),
                pl.BlockSpec((HB, nc, HEAD_DIM, HEAD_DIM), lambda i, hg: (hg, rev(i), 0, 0)),
                pl.BlockSpec((HB, nc, CHUNK, CHUNK), lambda i, hg: (hg, rev(i), 0, 0))]
    out_shape = [SDS((T, D), f32), SDS((T, D), f32), SDS((T, D), f32), SDS((T, D), bf16), SDS((T, LANES), f32),
                 SDS((1, LANES), f32), SDS((1, LANES), f32), SDS((1, LANES), f32)]
    out_specs = [col(0), col(0), col(0), col(0), pl.BlockSpec((tt, LANES), lambda i, hg: (rev(i), 0)),
                 vec(1), vec(1), vec(1)]
    return _pcall(body, out_shape=out_shape, grid=(nt, nhg), in_specs=in_specs, out_specs=out_specs,
                  scratch_shapes=[pltpu.VMEM((H, HEAD_DIM, HEAD_DIM), f32)],
                  compiler_params=_cparams(("arbitrary", "arbitrary")), name=name)(
        do, qkv, qkv, qkv, z, z, alog_vec, dtb_vec, nw, ssave, isave)


def colsum(a, name, tt=512):
    T, N = a.shape
    tt = min(tt, T)
    tn = _pick(N, 2048, LANES)

    def body(a_ref, o_ref):
        i = pl.program_id(1)
        s = jnp.sum(a_ref[...].astype(f32), axis=0, keepdims=True)

        @pl.when(i == 0)
        def _():
            o_ref[...] = s

        @pl.when(i > 0)
        def _():
            o_ref[...] += s

    return _pcall(body, out_shape=SDS((1, N), f32), grid=(N // tn, T // tt),
                  in_specs=[pl.BlockSpec((tt, tn), lambda j, i: (i, j))], out_specs=pl.BlockSpec((1, tn), lambda j, i: (0, j)),
                  compiler_params=_cparams(("parallel", "arbitrary")), name=name)(a)


def loss_and_grad(y, target, name, tt=256):
    T, D = y.shape
    tt = min(tt, T)

    def body(y_ref, t_ref, loss_ref, dy_ref):
        i = pl.program_id(0)
        e = y_ref[...] - t_ref[...]
        dy_ref[...] = e * (1.0 / D)
        part = jnp.sum(jnp.sum(e * e, axis=1, keepdims=True), axis=0, keepdims=True) * (0.5 / D)
        part = jnp.broadcast_to(part, (1, LANES))

        @pl.when(i == 0)
        def _():
            loss_ref[...] = part

        @pl.when(i > 0)
        def _():
            loss_ref[...] += part

    blk = pl.BlockSpec((tt, D), lambda i: (i, 0))
    return _pcall(body, out_shape=[SDS((1, LANES), f32), SDS((T, D), f32)], grid=(T // tt,), in_specs=[blk, blk],
                  out_specs=[pl.BlockSpec((1, LANES), lambda i: (0, 0)), blk],
                  compiler_params=_cparams(("arbitrary",)), name=name)(y, target)


def adamw(w, g, m, v, name):
    L, R, C = w.shape
    tr = R if R % 8 else _pick(R, max(8, (1 << 19) // C // 8 * 8), 8)
    c1 = 1.0 / (1.0 - ADAM_B1 ** ADAM_STEP)
    c2 = 1.0 / (1.0 - ADAM_B2 ** ADAM_STEP)

    def body(w_ref, g_ref, m_ref, v_ref, d_ref, nm_ref, nv_ref):
        gg = g_ref[...]
        nm = ADAM_B1 * m_ref[...] + (1.0 - ADAM_B1) * gg
        nv = ADAM_B2 * v_ref[...] + (1.0 - ADAM_B2) * (gg * gg)
        m_hat = nm * c1
        v_hat = nv * c2
        d_ref[...] = -ADAM_LR * (m_hat / (jnp.sqrt(v_hat) + ADAM_EPS) + ADAM_WD * w_ref[...])
        nm_ref[...] = nm
        nv_ref[...] = nv

    blk = pl.BlockSpec((1, tr, C), lambda l, r: (l, r, 0))
    shp = SDS((L, R, C), f32)
    return _pcall(body, out_shape=[shp, shp, shp], grid=(L, R // tr), in_specs=[blk] * 4, out_specs=[blk] * 3,
                  compiler_params=_cparams(("parallel", "parallel")), name=name)(w, g, m, v)


HBM_SPEC = pl.BlockSpec(memory_space=pltpu.HBM)


def _dma_sems(n):
    return pltpu.SemaphoreType.DMA((n,))


def all_gather(shards, name):
    n = len(shards)

    def body(*refs):
        x_refs, out_refs = refs[:n], refs[n:2 * n]
        send_sems, recv_sems, local_sems = refs[2 * n:]
        x, y, c = lax.axis_index("x"), lax.axis_index("y"), lax.axis_index("c")
        me, sibling = (x, y, c), (x, y, 1 - c)
        chips = [(1 - x, y), (x, 1 - y), (1 - x, 1 - y)]

        def slot(t, px, py, pc):
            return out_refs[t].at[4 * px + 2 * py + pc]

        def copy(t, k, block, to, src=None):
            return pltpu.make_async_remote_copy(src_ref=slot(t, *block) if src is None else src, dst_ref=slot(t, *block),
                                                send_sem=send_sems.at[7 * t + k], recv_sem=recv_sems.at[7 * t + k],
                                                device_id=to, device_id_type=MESH)

        mine = [pltpu.make_async_copy(x_refs[t], slot(t, *me), local_sems.at[t]) for t in range(n)]
        for cp in mine:
            cp.start()
        first = []
        for t in range(n):
            first.append(copy(t, 0, me, sibling, src=x_refs[t]))
            first += [copy(t, 1 + j, me, (*chip, c), src=x_refs[t]) for j, chip in enumerate(chips)]
        for cp in first:
            cp.start()
        passed = []
        for j, chip in enumerate(chips):
            for t in range(n):
                copy(t, 1 + j, (*chip, c), me).wait_recv()
                cp = copy(t, 4 + j, (*chip, c), sibling)
                cp.start()
                passed.append(cp)
        for t in range(n):
            copy(t, 0, sibling, me).wait_recv()
        for j, chip in enumerate(chips):
            for t in range(n):
                copy(t, 4 + j, (*chip, 1 - c), me).wait_recv()
        for cp in first + passed:
            cp.wait_send()
        for cp in mine:
            cp.wait()

    return _pcall(body, out_shape=[SDS((N_DEV,) + s.shape, s.dtype) for s in shards], in_specs=[HBM_SPEC] * n,
                  out_specs=[HBM_SPEC] * n, scratch_shapes=[_dma_sems(7 * n), _dma_sems(7 * n), _dma_sems(n)],
                  name=name)(*shards)


def exchange_sibling(parts, name):
    n = len(parts)

    def body(*refs):
        p_refs, r_refs = refs[:n], refs[n:2 * n]
        send_sems, recv_sems = refs[2 * n:]
        x, y, c = lax.axis_index("x"), lax.axis_index("y"), lax.axis_index("c")
        cps = []
        for t in range(n):
            for k in range(4):
                cps.append(pltpu.make_async_remote_copy(src_ref=p_refs[t].at[2 * k + 1 - c], dst_ref=r_refs[t].at[k],
                                                        send_sem=send_sems.at[4 * t + k], recv_sem=recv_sems.at[4 * t + k],
                                                        device_id=(x, y, 1 - c), device_id_type=MESH))
        for cp in cps:
            cp.start()
        for cp in cps:
            cp.wait_recv()
        for cp in cps:
            cp.wait_send()

    return _pcall(body, out_shape=[SDS((4,) + p.shape[1:], p.dtype) for p in parts], in_specs=[HBM_SPEC] * n,
                  out_specs=[HBM_SPEC] * n, scratch_shapes=[_dma_sems(4 * n), _dma_sems(4 * n)], name=name)(*parts)


def exchange_chips(s1s, name):
    n = len(s1s)

    def body(*refs):
        s_refs, r_refs = refs[:n], refs[n:2 * n]
        send_sems, recv_sems, local_sems = refs[2 * n:]
        x, y, c = lax.axis_index("x"), lax.axis_index("y"), lax.axis_index("c")
        my_chip = 2 * x + y
        chips = [(1 - x, y), (x, 1 - y), (1 - x, 1 - y)]
        mine = [pltpu.make_async_copy(s_refs[t].at[my_chip], r_refs[t].at[my_chip], local_sems.at[t]) for t in range(n)]
        for cp in mine:
            cp.start()
        cps = []
        for t in range(n):
            for j, (px, py) in enumerate(chips):
                cps.append(pltpu.make_async_remote_copy(src_ref=s_refs[t].at[2 * px + py], dst_ref=r_refs[t].at[my_chip],
                                                        send_sem=send_sems.at[3 * t + j], recv_sem=recv_sems.at[3 * t + j],
                                                        device_id=(px, py, c), device_id_type=MESH))
        for cp in cps:
            cp.start()
        for t in range(n):
            for j, (px, py) in enumerate(chips):
                pltpu.make_async_remote_copy(src_ref=s_refs[t].at[my_chip], dst_ref=r_refs[t].at[2 * px + py],
                                             send_sem=send_sems.at[3 * t + j], recv_sem=recv_sems.at[3 * t + j],
                                             device_id=(px, py, c), device_id_type=MESH).wait_recv()
        for cp in cps:
            cp.wait_send()
        for cp in mine:
            cp.wait()

    return _pcall(body, out_shape=[SDS(s.shape, s.dtype) for s in s1s], in_specs=[HBM_SPEC] * n, out_specs=[HBM_SPEC] * n,
                  scratch_shapes=[_dma_sems(3 * n), _dma_sems(3 * n), _dma_sems(n)], name=name)(*s1s)


def _rows_tile(r, c, itemsize):
    return _pick(r, max(16, (1 << 20) // (c * itemsize) // 16 * 16), 16)


def pair_add(part, got, name):
    _, r, c = part.shape
    tr = _rows_tile(r, c, 2)
    core = lax.axis_index("c").reshape(1).astype(jnp.int32)

    def body(core_ref, p_ref, g_ref, o_ref):
        o_ref[...] = (p_ref[...].astype(f32) + g_ref[...].astype(f32)).astype(bf16)

    blk = pl.BlockSpec((None, tr, c), lambda k, i, core_ref: (k, i, 0))
    gs = pltpu.PrefetchScalarGridSpec(
        num_scalar_prefetch=1, grid=(4, r // tr),
        in_specs=[pl.BlockSpec((None, None, tr, c), lambda k, i, core_ref: (k, core_ref[0], i, 0)), blk], out_specs=blk)
    return _pcall(body, out_shape=SDS((4, r, c), bf16), grid_spec=gs, compiler_params=_cparams(("parallel", "parallel")),
                  name=name)(core, part.reshape(4, 2, r, c), got)


def slot_sum(a, name):
    S, r, c = a.shape
    tr = _rows_tile(r, c, a.dtype.itemsize * S)

    def body(a_ref, o_ref):
        acc = a_ref[0].astype(f32)
        for s in range(1, S):
            acc = acc + a_ref[s].astype(f32)
        o_ref[...] = acc

    return _pcall(body, out_shape=SDS((r, c), f32), grid=(r // tr,), in_specs=[pl.BlockSpec((S, tr, c), lambda i: (0, i, 0))],
                  out_specs=pl.BlockSpec((tr, c), lambda i: (i, 0)), compiler_params=_cparams(("parallel",)), name=name)(a)


def reduce_scatter(parts, name):
    got = exchange_sibling(parts, name + "_c")
    s1 = [pair_add(p, g, f"{name}_add{t}") for t, (p, g) in enumerate(zip(parts, got))]
    got2 = exchange_chips(s1, name + "_xy")
    return [slot_sum(g, f"{name}_sum{t}") for t, g in enumerate(got2)]


def _to_pack(flat, dtype):
    n = flat.shape[-1]
    unit = 16 * PACK_COLS
    padded = -(-n // unit) * unit
    return jnp.pad(flat.astype(dtype), (0, padded - n)).reshape(padded // PACK_COLS, PACK_COLS)


BIG = ("w_in", "w_conv_proj", "w_gdn_proj", "w_out", "w_ffn_in", "w_ffn_out")
COL_SHARDED = ("w_in", "w_ffn_in", "conv_dw_w", "short_conv_w")
SMALL = ("b_in", "conv_dw_b", "conv_ln_g", "conv_ln_b", "b_conv_proj", "a_log", "dt_bias", "gdn_norm_w",
         "ln1_g", "ln1_b", "ln2_g", "ln2_b")
CONVW = ("conv_dw_w", "short_conv_w")
ORDER = ("w_in", "b_in", "conv_dw_w", "conv_dw_b", "conv_ln_g", "conv_ln_b", "w_conv_proj", "b_conv_proj",
         "short_conv_w", "a_log", "dt_bias", "gdn_norm_w", "w_gdn_proj", "w_out", "ln1_g", "ln1_b",
         "w_ffn_in", "w_ffn_out", "ln2_g", "ln2_b")


def _full_from_gathered(g, name):
    if name in COL_SHARDED:
        return jnp.moveaxis(g, 0, 1).reshape(g.shape[1], N_DEV * g.shape[2])
    return g.reshape(N_DEV * g.shape[1], g.shape[2])


def _w_in_perm(w, D, H):
    pad = jnp.zeros(w.shape[:-1] + (2 * LANES - 2 * H,), w.dtype)
    return jnp.concatenate([w[..., :6 * D], w[..., 6 * D + 2 * H:], w[..., 6 * D:6 * D + 2 * H], pad], axis=-1)


def _w_in_unperm(w, D, H):
    return jnp.concatenate([w[..., :6 * D], w[..., 8 * D:8 * D + 2 * H], w[..., 6 * D:8 * D]], axis=-1)


def kernel(x, w_in, b_in, conv_dw_w, conv_dw_b, conv_ln_g, conv_ln_b, w_conv_proj, b_conv_proj, short_conv_w, a_log, dt_bias, gdn_norm_w, w_gdn_proj, w_out, ln1_g, ln1_b, w_ffn_in, w_ffn_out, ln2_g, ln2_b, loss_target, m_w_in, m_b_in, m_conv_dw_w, m_conv_dw_b, m_conv_ln_g, m_conv_ln_b, m_w_conv_proj, m_b_conv_proj, m_short_conv_w, m_a_log, m_dt_bias, m_gdn_norm_w, m_w_gdn_proj, m_w_out, m_ln1_g, m_ln1_b, m_w_ffn_in, m_w_ffn_out, m_ln2_g, m_ln2_b, v_w_in, v_b_in, v_conv_dw_w, v_conv_dw_b, v_conv_ln_g, v_conv_ln_b, v_w_conv_proj, v_b_conv_proj, v_short_conv_w, v_a_log, v_dt_bias, v_gdn_norm_w, v_w_gdn_proj, v_w_out, v_ln1_g, v_ln1_b, v_w_ffn_in, v_w_ffn_out, v_ln2_g, v_ln2_b):
    W = dict(w_in=w_in, b_in=b_in, conv_dw_w=conv_dw_w, conv_dw_b=conv_dw_b, conv_ln_g=conv_ln_g, conv_ln_b=conv_ln_b,
             w_conv_proj=w_conv_proj, b_conv_proj=b_conv_proj, short_conv_w=short_conv_w, a_log=a_log, dt_bias=dt_bias,
             gdn_norm_w=gdn_norm_w, w_gdn_proj=w_gdn_proj, w_out=w_out, ln1_g=ln1_g, ln1_b=ln1_b, w_ffn_in=w_ffn_in,
             w_ffn_out=w_ffn_out, ln2_g=ln2_g, ln2_b=ln2_b)
    MO = dict(w_in=m_w_in, b_in=m_b_in, conv_dw_w=m_conv_dw_w, conv_dw_b=m_conv_dw_b, conv_ln_g=m_conv_ln_g,
              conv_ln_b=m_conv_ln_b, w_conv_proj=m_w_conv_proj, b_conv_proj=m_b_conv_proj, short_conv_w=m_short_conv_w,
              a_log=m_a_log, dt_bias=m_dt_bias, gdn_norm_w=m_gdn_norm_w, w_gdn_proj=m_w_gdn_proj, w_out=m_w_out,
              ln1_g=m_ln1_g, ln1_b=m_ln1_b, w_ffn_in=m_w_ffn_in, w_ffn_out=m_w_ffn_out, ln2_g=m_ln2_g, ln2_b=m_ln2_b)
    VO = dict(w_in=v_w_in, b_in=v_b_in, conv_dw_w=v_conv_dw_w, conv_dw_b=v_conv_dw_b, conv_ln_g=v_conv_ln_g,
              conv_ln_b=v_conv_ln_b, w_conv_proj=v_w_conv_proj, b_conv_proj=v_b_conv_proj, short_conv_w=v_short_conv_w,
              a_log=v_a_log, dt_bias=v_dt_bias, gdn_norm_w=v_gdn_norm_w, w_gdn_proj=v_w_gdn_proj, w_out=v_w_out,
              ln1_g=v_ln1_g, ln1_b=v_ln1_b, w_ffn_in=v_w_ffn_in, w_ffn_out=v_w_ffn_out, ln2_g=v_ln2_g, ln2_b=v_ln2_b)

    _, T, D = x.shape
    DEPTH = w_in.shape[0]
    H = D // HEAD_DIM
    F = w_ffn_out.shape[1] * N_DEV
    alpha = (2.0 * DEPTH) ** 0.25
    lnres_fn = make_lnres_fn(alpha)
    NA = 8 * D + 2 * LANES
    TT = min(256, T)
    dev = 4 * lax.axis_index("x") + 2 * lax.axis_index("y") + lax.axis_index("c")

    cw_sizes = [(n, W[n].shape[1:]) for n in CONVW]
    cw_flat = jnp.concatenate([W[n].reshape(-1) for n in CONVW])
    cw_g = all_gather([_to_pack(cw_flat, f32)], "ag_convw")[0].reshape(N_DEV, -1)
    conv_full, off = {}, 0
    for n in CONVW:
        sz = W[n].size
        blk = cw_g[:, off:off + sz].reshape((N_DEV,) + W[n].shape)
        conv_full[n] = jnp.moveaxis(blk, 0, 2).reshape(W[n].shape[0], W[n].shape[1], -1)
        off += sz

    def gather_layer(l):
        g = dict(zip(BIG, all_gather([W[n][l].astype(bf16) for n in BIG], f"ag_w{l}")))
        out = {n: g[n].reshape(-1, g[n].shape[2]) for n in BIG if n not in COL_SHARDED}
        out["w_ffn_in"] = g["w_ffn_in"]
        out["w_in"] = _w_in_perm(_full_from_gathered(g["w_in"], "w_in"), D, H)
        return out

    def lane_vec(v, off=0):
        return jnp.pad(v, (off, LANES - off - v.shape[0])).reshape(1, LANES)

    def row(v):
        return v.reshape(1, -1)

    h32 = x.reshape(T, D)
    h16 = h32.astype(bf16)
    saved = []
    for l in range(DEPTH):
        G = gather_layer(l)
        b_all = _w_in_perm(b_in[l], D, H)
        alog_vec, dtb_vec = lane_vec(a_log[l], H), lane_vec(dt_bias[l], H)
        nw = row(gdn_norm_w[l])
        z = matmul(h16, G["w_in"], "nn", f"l{l}_mm_in", bias=b_all)
        c0, = rowwise(glu_fn, [(z, 0, D), (z, 1, D)], [], [(f32,)], TT, f"l{l}_glu")
        c1 = conv_fwd(c0, 0, conv_full["conv_dw_w"][l], 0, D, CONV_WIDTH, row(conv_dw_b[l]), f"l{l}_conv")
        c3, = rowwise(lnsilu_fn, [(c1, 0, D)], [row(conv_ln_g[l]), row(conv_ln_b[l])], [(bf16,)], TT, f"l{l}_lnsilu")
        yc = matmul(c3, G["w_conv_proj"], "nn", f"l{l}_mm_cp", bias=b_conv_proj[l])
        qkv = conv_fwd(z, 2 * D, conv_full["short_conv_w"][l], 0, 3 * D, SHORT_CONV, None, f"l{l}_sconv")
        og, ssave, isave = gdn_fwd(qkv, z, 5 * D, 8 * D, alog_vec, dtb_vec, nw, f"l{l}_gdn")
        yg = matmul(og, G["w_gdn_proj"], "nn", f"l{l}_mm_gp")
        m, = rowwise(merge_fn, [(z, 6, D), (z, 7, D), (yc, 0, D), (yg, 0, D)], [], [(bf16,)], TT, f"l{l}_merge")
        mix = matmul(m, G["w_out"], "nn", f"l{l}_mm_out")
        x1_32, x1_16 = rowwise(lnres_fn, [(h32, 0, D), (mix, 0, D)], [row(ln1_g[l]), row(ln1_b[l])], [(f32, bf16)], TT, f"l{l}_ln1")
        hf = matmul(x1_16, G["w_ffn_in"], "nn", f"l{l}_mm_fi", b_blocked=True)
        act, = rowwise(swiglu_fn, [(hf, 0, F), (hf, 1, F)], [], [(bf16,)], min(128, T), f"l{l}_swiglu")
        ff = matmul(act, G["w_ffn_out"], "nn", f"l{l}_mm_fo")
        x2_32, x2_16 = rowwise(lnres_fn, [(x1_32, 0, D), (ff, 0, D)], [row(ln2_g[l]), row(ln2_b[l])], [(f32, bf16)], TT, f"l{l}_ln2")
        saved.append(dict(G=G, h32=h32, h16=h16, z=z, c0=c0, c1=c1, c3=c3, yc=yc, qkv=qkv, ssave=ssave, isave=isave, og=og, yg=yg, m=m,
                          mix=mix, x1_32=x1_32, x1_16=x1_16, hf=hf, act=act, ff=ff, alog_vec=alog_vec, dtb_vec=dtb_vec, nw=nw))
        h32, h16 = x2_32, x2_16

    loss_vec, dy = loss_and_grad(h32, loss_target.reshape(T, D), "loss")
    loss = lax.psum(loss_vec[0, 0], AXES)

    gsmall = {n: [None] * DEPTH for n in SMALL + CONVW}
    gbig = {n: [None] * DEPTH for n in BIG}
    dh = [dy]
    for l in reversed(range(DEPTH)):
        s = saved[l]
        G = s["G"]
        TB = min(128, T)
        d_x1r, d_ff, dg2, db2 = rowwise_vjp(lnres_fn, [(s["x1_32"], 0, D), (s["ff"], 0, D)], [row(ln2_g[l]), row(ln2_b[l])],
                                            [dh], [f32, bf16], TB, f"l{l}_ln2_b")
        d_act = matmul(d_ff, G["w_ffn_out"], "nt", f"l{l}_mm_fo_dx")
        dw_fo = matmul(s["act"], d_ff, "tn", f"l{l}_mm_fo_dw", tk_cap=1024, out_dtype=bf16)
        d_gate, d_up = rowwise_vjp(swiglu_fn, [(s["hf"], 0, F), (s["hf"], 1, F)], [], [[d_act]], [bf16, bf16], min(64, T), f"l{l}_swiglu_b")
        d_hf = jnp.concatenate([d_gate, d_up], axis=1)
        d_x1m = matmul(d_hf, G["w_ffn_in"], "nt", f"l{l}_mm_fi_dx", b_blocked=True)
        dw_fi = matmul(s["x1_16"], d_hf, "tn", f"l{l}_mm_fi_dw", tk_cap=1024, out_dtype=bf16, out_blocked=True)
        d_hr, d_mix, dg1, db1 = rowwise_vjp(lnres_fn, [(s["h32"], 0, D), (s["mix"], 0, D)], [row(ln1_g[l]), row(ln1_b[l])],
                                            [[d_x1r, d_x1m]], [f32, bf16], TB, f"l{l}_ln1_b")
        d_m = matmul(d_mix, G["w_out"], "nt", f"l{l}_mm_out_dx")
        dw_out = matmul(s["m"], d_mix, "tn", f"l{l}_mm_out_dw", tk_cap=1024, out_dtype=bf16)
        d_ga, d_gb, d_yc, d_yg = rowwise_vjp(merge_fn, [(s["z"], 6, D), (s["z"], 7, D), (s["yc"], 0, D), (s["yg"], 0, D)], [],
                                             [[d_m]], [bf16] * 4, TB, f"l{l}_merge_b")
        d_c3 = matmul(d_yc, G["w_conv_proj"], "nt", f"l{l}_mm_cp_dx")
        dw_cp = matmul(s["c3"], d_yc, "tn", f"l{l}_mm_cp_dw", tk_cap=1024, out_dtype=bf16)
        db_cp = colsum(d_yc, f"l{l}_cs_cp")
        d_c1, dcg, dcb = rowwise_vjp(lnsilu_fn, [(s["c1"], 0, D)], [row(conv_ln_g[l]), row(conv_ln_b[l])], [[d_c3]], [f32], TB, f"l{l}_lnsilu_b")
        d_c0, dw31, db31 = conv_bwd(d_c1, s["c0"], 0, conv_full["conv_dw_w"][l], 0, D, CONV_WIDTH, f32, f"l{l}_conv_b")
        d_glu_a, d_glu_b = rowwise_vjp(glu_fn, [(s["z"], 0, D), (s["z"], 1, D)], [], [[d_c0]], [bf16, bf16], TB, f"l{l}_glu_b")
        d_og = matmul(d_yg, G["w_gdn_proj"], "nt", f"l{l}_mm_gp_dx")
        dw_gp = matmul(s["og"], d_yg, "tn", f"l{l}_mm_gp_dw", tk_cap=1024, out_dtype=bf16)
        dq, dk, dv, d_zg, d_zs, dalog, ddtb, dnw = gdn_bwd(d_og, s["qkv"], s["z"], 5 * D, 8 * D, s["alog_vec"], s["dtb_vec"], s["nw"],
                                                            s["ssave"], s["isave"], f"l{l}_gdn_b")
        dxs, dwss = [], []
        for sec, dsec in enumerate((dq, dk, dv)):
            dxp, dwp, _ = conv_bwd(dsec, s["z"], (2 + sec) * D, conv_full["short_conv_w"][l], sec * D, D, SHORT_CONV, bf16,
                                   f"l{l}_sconv_b{sec}")
            dxs.append(dxp)
            dwss.append(dwp)
        dz = jnp.concatenate([d_glu_a, d_glu_b] + dxs + [d_zg, d_ga, d_gb, d_zs.astype(bf16), jnp.zeros((T, LANES), bf16)], axis=1)
        d_hm = matmul(dz, G["w_in"], "nt", f"l{l}_mm_in_dx")
        dw_in = matmul(s["h16"], dz, "tn", f"l{l}_mm_in_dw", tk_cap=1024, out_dtype=bf16)
        db_all = colsum(dz, f"l{l}_cs_in")
        dh = [d_hr, d_hm]

        gsmall["b_in"][l] = _w_in_unperm(db_all[0], D, H)
        gsmall["conv_dw_b"][l] = db31[0]
        gsmall["conv_ln_g"][l], gsmall["conv_ln_b"][l] = dcg[0], dcb[0]
        gsmall["b_conv_proj"][l] = db_cp[0]
        gsmall["a_log"][l], gsmall["dt_bias"][l] = dalog[0, H:2 * H], ddtb[0, H:2 * H]
        gsmall["gdn_norm_w"][l] = dnw[0]
        gsmall["ln1_g"][l], gsmall["ln1_b"][l], gsmall["ln2_g"][l], gsmall["ln2_b"][l] = dg1[0], db1[0], dg2[0], db2[0]
        gsmall["conv_dw_w"][l] = dw31
        gsmall["short_conv_w"][l] = jnp.concatenate(dwss, axis=1)

        dw_in_u = _w_in_unperm(dw_in, D, H)
        blocks = dict(w_in=jnp.moveaxis(dw_in_u.reshape(D, N_DEV, -1), 1, 0), w_ffn_in=dw_fi)
        for n, dw in (("w_conv_proj", dw_cp), ("w_gdn_proj", dw_gp), ("w_out", dw_out), ("w_ffn_out", dw_fo)):
            blocks[n] = dw.reshape(N_DEV, -1, dw.shape[1])
        for n, red in zip(BIG, reduce_scatter([blocks[n] for n in BIG], f"rs{l}")):
            gbig[n][l] = red

    grad_x, = rowwise(add_fn, [(dh[0], 0, D), (dh[1], 0, D)], [], [(f32,)], TT, "grad_x")
    grad_x = grad_x.reshape(1, T, D)

    small_flat = jnp.concatenate([jnp.stack(gsmall[n]).reshape(-1) for n in SMALL + CONVW])
    sg = all_gather([_to_pack(small_flat, f32)], "ag_small")[0]
    small_tot = slot_sum(sg, "small_sum").reshape(-1)
    grads, off = {}, 0
    for n in SMALL:
        grads[n] = small_tot[off:off + W[n].size].reshape(W[n].shape)
        off += W[n].size
    for n in CONVW:
        L_, K_, c_ = W[n].shape
        fullg = small_tot[off:off + L_ * K_ * c_ * N_DEV].reshape(L_, K_, c_ * N_DEV)
        grads[n] = lax.dynamic_slice_in_dim(fullg, dev * c_, c_, axis=2)
        off += L_ * K_ * c_ * N_DEV
    for n in BIG:
        grads[n] = jnp.stack(gbig[n])

    delta, new_m, new_v = {}, {}, {}
    for n in BIG + CONVW:
        delta[n], new_m[n], new_v[n] = adamw(W[n], grads[n], MO[n], VO[n], f"adamw_{n}")
    pk = lambda d: _to_pack(jnp.concatenate([d[n].reshape(-1) for n in SMALL]), f32)[None]
    ds, ms, vs = adamw(pk(W), pk(grads), pk(MO), pk(VO), "adamw_small")
    off = 0
    for n in SMALL:
        sl = lambda a: a.reshape(-1)[off:off + W[n].size].reshape(W[n].shape)
        delta[n], new_m[n], new_v[n] = sl(ds), sl(ms), sl(vs)
        off += W[n].size

    return (loss, grad_x, *[grads[n] for n in ORDER], *[delta[n] for n in ORDER],
            *[new_m[n] for n in ORDER], *[new_v[n] for n in ORDER])
```

```python
import functools
import math

import jax
import jax.numpy as jnp
from jax import lax
from jax.experimental import pallas as pl
from jax.experimental.pallas import tpu as pltpu
from jax.experimental.pallas import tpu_sc as plsc

f32, bf16 = jnp.float32, jnp.bfloat16
SDS = jax.ShapeDtypeStruct
MESH = pl.DeviceIdType.MESH
AXES = ("x", "y", "c")
N_DEV = 8

CONV_WIDTH = 31
SHORT_CONV = 4
HEAD_DIM = 128
CHUNK = 64
LN_EPS = 1e-5
ADAM_LR, ADAM_B1, ADAM_B2, ADAM_EPS, ADAM_WD, ADAM_STEP = 0.001, 0.9, 0.999, 1e-08, 0.01, 10

LANES = 128
HALO = 32
PACK_COLS = 1024
VMEM_LIMIT = 56 * 1024 * 1024


def _pcall(body, **kw):
    return pl.pallas_call(body, **kw)


def _cparams(sem=None):
    return pltpu.CompilerParams(dimension_semantics=sem, vmem_limit_bytes=VMEM_LIMIT)


def _pick(n, cap, mult):
    if n <= cap:
        return n
    for t in range(cap - cap % mult, 0, -mult):
        if n % t == 0:
            return t
    raise ValueError(f"no tile for {n} under {cap} in steps of {mult}")


def matmul(a, b, mode, name, bias=None, out_dtype=f32, tm_cap=1024, tn_cap=1280, tk_cap=2048,
           b_blocked=False, out_blocked=False):
    if b_blocked:
        nb, br, bc = b.shape
        b2 = (br, nb * bc)
    else:
        b2 = b.shape
    if mode == "nn":
        (M, K), (K2, N) = a.shape, b2
    elif mode == "nt":
        (M, K), (N, K2) = a.shape, b2
    else:
        (K, M), (K2, N) = a.shape, b2
    assert K == K2, (a.shape, b.shape, mode)
    tm = _pick(M, tm_cap, 256 if M % 256 == 0 else 8)
    tn = _pick(N, tn_cap, 256 if N % 256 == 0 else LANES)
    tk = _pick(K, tk_cap, 256 if K % 256 == 0 else LANES)
    if b_blocked and mode == "nn":
        tn = N // N_DEV
    if b_blocked and mode == "nt":
        tk = K // N_DEV
    if out_blocked:
        tn = N // N_DEV
    nm, nn, nk = M // tm, N // tn, K // tk
    dims = {"nn": (((1,), (0,)), ((), ())), "nt": (((1,), (1,)), ((), ())), "tn": (((0,), (0,)), ((), ()))}[mode]
    has_bias = bias is not None

    def body(*refs):
        a_ref, b_ref = refs[0], refs[1]
        bias_ref = refs[2] if has_bias else None
        o_ref = refs[2 + has_bias]
        prod = lax.dot_general(a_ref[...], b_ref[...], dims, preferred_element_type=f32)

        def finish(acc):
            if has_bias:
                acc = acc + bias_ref[...]
            o_ref[...] = acc.astype(out_dtype)

        if nk == 1:
            finish(prod)
        else:
            acc_ref = refs[3 + has_bias]
            k = pl.program_id(2)

            @pl.when(k == 0)
            def _():
                acc_ref[...] = prod

            @pl.when(jnp.logical_and(k > 0, k < nk - 1))
            def _():
                acc_ref[...] += prod

            @pl.when(k == nk - 1)
            def _():
                finish(acc_ref[...] + prod)

    a_bytes, b_bytes = M * K, N * K
    m_outer = a_bytes >= b_bytes
    if m_outer:
        grid = (nm, nn, nk)
        gi = lambda i, j, k: (i, j, k)
    else:
        grid = (nn, nm, nk)
        gi = lambda j, i, k: (i, j, k)

    def amap(*g):
        i, j, k = gi(*g)
        return (k, i) if mode == "tn" else (i, k)

    def bmap(*g):
        i, j, k = gi(*g)
        return (j, k) if mode == "nt" else (k, j)

    def omap(*g):
        i, j, k = gi(*g)
        return (i, j)

    def biasmap(*g):
        i, j, k = gi(*g)
        return (0, j)

    def bmap_blocked(*g):
        i, j, k = gi(*g)
        return (k, j, 0) if mode == "nt" else (j, k, 0)

    def omap_blocked(*g):
        i, j, k = gi(*g)
        return (j, i, 0)

    b_block = (tn, tk) if mode == "nt" else (tk, tn)
    in_specs = [pl.BlockSpec((tk, tm) if mode == "tn" else (tm, tk), amap),
                pl.BlockSpec((None,) + b_block, bmap_blocked) if b_blocked else pl.BlockSpec(b_block, bmap)]
    args = [a, b]
    if has_bias:
        in_specs.append(pl.BlockSpec((1, tn), biasmap))
        args.append(bias.reshape(1, N).astype(f32))
    if out_blocked:
        out_shape, out_spec = SDS((N_DEV, M, tn), out_dtype), pl.BlockSpec((None, tm, tn), omap_blocked)
    else:
        out_shape, out_spec = SDS((M, N), out_dtype), pl.BlockSpec((tm, tn), omap)
    return _pcall(body, out_shape=out_shape, grid=grid, in_specs=in_specs,
                  out_specs=out_spec,
                  scratch_shapes=[pltpu.VMEM((tm, tn), f32)] if nk > 1 else [],
                  compiler_params=_cparams(("parallel", "parallel", "arbitrary")), name=name)(*args)


def _row_specs(rows, tt):
    specs, args = [], []
    for arr, cb, width in rows:
        specs.append(pl.BlockSpec((tt, width), lambda i, cb=cb: (i, cb)))
        args.append(arr)
    return specs, args


def _param_specs(params):
    return [pl.BlockSpec(p.shape, lambda i: (0, 0)) for p in params]


def rowwise(fn, rows, params, out_dtypes, tt, name):
    T = rows[0][0].shape[0]
    nr, npar = len(rows), len(params)
    blocks = [SDS((tt, w), f32) for _, _, w in rows] + [SDS(p.shape, f32) for p in params]
    outs = jax.eval_shape(fn, *blocks)
    out_shape, out_specs = [], []
    for o, dts in zip(outs, out_dtypes):
        for dt in dts:
            out_shape.append(SDS((T, o.shape[1]), dt))
            out_specs.append(pl.BlockSpec((tt, o.shape[1]), lambda i: (i, 0)))

    def body(*refs):
        xs = [r[...].astype(f32) for r in refs[:nr + npar]]
        res = fn(*xs)
        k = nr + npar
        for o, dts in zip(res, out_dtypes):
            for dt in dts:
                refs[k][...] = o.astype(dt)
                k += 1

    rspecs, rargs = _row_specs(rows, tt)
    return _pcall(body, out_shape=out_shape, grid=(T // tt,), in_specs=rspecs + _param_specs(params),
                  out_specs=out_specs, compiler_params=_cparams(("parallel",)), name=name)(*rargs, *params)


def rowwise_vjp(fn, rows, params, cots, d_dtypes, tt, name):
    T = rows[0][0].shape[0]
    nr, npar = len(rows), len(params)
    ncot = [len(c) for c in cots]
    flat_cots = [c for cs in cots for c in cs]

    def body(*refs):
        i = pl.program_id(0)
        xs = [r[...].astype(f32) for r in refs[:nr + npar]]
        k = nr + npar
        cs = []
        for n in ncot:
            tot = refs[k][...].astype(f32)
            for r in refs[k + 1:k + n]:
                tot = tot + r[...].astype(f32)
            cs.append(tot)
            k += n
        _, pull = jax.vjp(fn, *xs)
        grads = pull(tuple(cs))
        for g, dt in zip(grads[:nr], d_dtypes):
            refs[k][...] = g.astype(dt)
            k += 1
        for g in grads[nr:]:
            ref = refs[k]
            k += 1

            @pl.when(i == 0)
            def _(ref=ref, g=g):
                ref[...] = g

            @pl.when(i > 0)
            def _(ref=ref, g=g):
                ref[...] += g

    rspecs, rargs = _row_specs(rows, tt)
    cot_specs = [pl.BlockSpec((tt, c.shape[1]), lambda i: (i, 0)) for c in flat_cots]
    out_shape = [SDS((T, w), dt) for (_, _, w), dt in zip(rows, d_dtypes)] + [SDS(p.shape, f32) for p in params]
    out_specs = [pl.BlockSpec((tt, w), lambda i: (i, 0)) for _, _, w in rows] + _param_specs(params)
    return _pcall(body, out_shape=out_shape, grid=(T // tt,), in_specs=rspecs + _param_specs(params) + cot_specs,
                  out_specs=out_specs, compiler_params=_cparams(("arbitrary",)), name=name)(*rargs, *params, *flat_cots)


def _sigmoid(x):
    return jax.nn.sigmoid(x)


def _silu(x):
    return x * jax.nn.sigmoid(x)


def _softplus(x):
    return jnp.maximum(x, 0.0) + jnp.log(1.0 + jnp.exp(-jnp.abs(x)))


def _layer_norm(x, g, b):
    mu = jnp.mean(x, axis=-1, keepdims=True)
    xc = x - mu
    var = jnp.mean(xc * xc, axis=-1, keepdims=True)
    return xc * lax.rsqrt(var + LN_EPS) * g + b


def glu_fn(a, b):
    return (a * _sigmoid(b),)


def lnsilu_fn(c, g, b):
    return (_silu(_layer_norm(c, g, b)),)


def merge_fn(ga, gb, yc, yg):
    return (_sigmoid(ga) * yc + _sigmoid(gb) * yg,)


def swiglu_fn(gate, up):
    return (_silu(gate) * up,)


def make_lnres_fn(alpha):
    def lnres_fn(h, y, g, b):
        return (_layer_norm(alpha * h + y, g, b),)
    return lnres_fn


def add_fn(a, b):
    return (a + b,)


CONV_ROWS = 64


def _tap_groups(K, first_row):
    groups = {}
    for j in range(K):
        groups.setdefault((first_row + j) % 8, []).append(j)
    out = []
    for taps in groups.values():
        start = first_row + taps[0]
        out.append((start, 8 * (len(taps) - 1), [(j, first_row + j - start) for j in taps]))
    return out


def conv_fwd(x, x_col0, w, w_col0, width, K, bias, name, tt=512, tc=256):
    T = x.shape[0]
    tt = min(tt, T)
    assert width % tc == 0 and x_col0 % tc == 0 and w_col0 % tc == 0 and tt % HALO == 0 and tt % CONV_ROWS == 0
    hb = tt // HALO
    xb, wb = x_col0 // tc, w_col0 // tc
    has_bias = bias is not None
    groups = _tap_groups(K, HALO - (K - 1))

    def body(*refs):
        x_ref, halo_ref, w_ref = refs[:3]
        b_ref = refs[3] if has_bias else None
        o_ref, ext_ref = refs[3 + has_bias], refs[4 + has_bias]
        i = pl.program_id(1)
        ext_ref[pl.ds(0, HALO), :] = jnp.where(i > 0, halo_ref[...], 0.0)
        ext_ref[pl.ds(HALO, tt), :] = x_ref[...]
        for r in range(tt // CONV_ROWS):
            acc = jnp.zeros((CONV_ROWS, tc), f32)
            for start, extra, taps in groups:
                win = ext_ref[pl.ds(r * CONV_ROWS + start, CONV_ROWS + extra), :]
                for j, off in taps:
                    acc = acc + w_ref[j:j + 1, :] * win[off:off + CONV_ROWS]
            if has_bias:
                acc = acc + b_ref[...]
            o_ref[pl.ds(r * CONV_ROWS, CONV_ROWS), :] = acc

    in_specs = [pl.BlockSpec((tt, tc), lambda cb, i: (i, xb + cb)),
                pl.BlockSpec((HALO, tc), lambda cb, i: (jnp.maximum(i * hb - 1, 0), xb + cb)),
                pl.BlockSpec((K, tc), lambda cb, i: (0, wb + cb))]
    args = [x, x, w]
    if has_bias:
        in_specs.append(pl.BlockSpec((1, tc), lambda cb, i: (0, cb)))
        args.append(bias)
    return _pcall(body, out_shape=SDS((T, width), f32), grid=(width // tc, T // tt), in_specs=in_specs,
                  out_specs=pl.BlockSpec((tt, tc), lambda cb, i: (i, cb)),
                  scratch_shapes=[pltpu.VMEM((tt + HALO, tc), f32)],
                  compiler_params=_cparams(("parallel", "arbitrary")), name=name)(*args)


def conv_bwd(dy, x, x_col0, w, w_col0, width, K, dx_dtype, name, tt=512, tc=256):
    T = x.shape[0]
    tt = min(tt, T)
    hb = tt // HALO
    nt = T // tt
    xb, wb = x_col0 // tc, w_col0 // tc
    x_groups = _tap_groups(K, HALO - (K - 1))
    dy_groups = [(start, extra, [(K - 1 - o, off) for o, off in taps]) for start, extra, taps in _tap_groups(K, 0)]
    RC = CONV_ROWS

    def body(dy_ref, dyn_ref, x_ref, halo_ref, w_ref, dx_ref, dw_ref, db_ref, xext_ref, dyext_ref, dwacc_ref, dbacc_ref):
        i = pl.program_id(1)

        @pl.when(i == 0)
        def _():
            dwacc_ref[...] = jnp.zeros_like(dwacc_ref)
            dbacc_ref[...] = jnp.zeros_like(dbacc_ref)

        xext_ref[pl.ds(0, HALO), :] = jnp.where(i > 0, halo_ref[...], 0.0)
        xext_ref[pl.ds(HALO, tt), :] = x_ref[...]
        dyext_ref[pl.ds(0, tt), :] = dy_ref[...].astype(f32)
        dyext_ref[pl.ds(tt, HALO), :] = jnp.where(i < nt - 1, dyn_ref[...].astype(f32), 0.0)

        def fold(p):
            return jnp.sum(p.reshape(RC // 8, 8, tc), axis=0)

        for r in range(tt // RC):
            acc = jnp.zeros((RC, tc), f32)
            for start, extra, taps in dy_groups:
                win = dyext_ref[pl.ds(r * RC + start, RC + extra), :]
                for j, off in taps:
                    acc = acc + w_ref[j:j + 1, :] * win[off:off + RC]
            dx_ref[pl.ds(r * RC, RC), :] = acc.astype(dx_dtype)
            dyc = dyext_ref[pl.ds(r * RC, RC), :]
            dbacc_ref[...] += fold(dyc)
            for start, extra, taps in x_groups:
                win = xext_ref[pl.ds(r * RC + start, RC + extra), :]
                for j, off in taps:
                    dwacc_ref[j] += fold(dyc * win[off:off + RC])

        @pl.when(i == nt - 1)
        def _():
            dw_ref[...] = jnp.sum(dwacc_ref[...], axis=1)
            db_ref[...] = jnp.sum(dbacc_ref[...], axis=0, keepdims=True)

    in_specs = [pl.BlockSpec((tt, tc), lambda cb, i: (i, cb)),
                pl.BlockSpec((HALO, tc), lambda cb, i: (jnp.minimum((i + 1) * hb, nt * hb - 1), cb)),
                pl.BlockSpec((tt, tc), lambda cb, i: (i, xb + cb)),
                pl.BlockSpec((HALO, tc), lambda cb, i: (jnp.maximum(i * hb - 1, 0), xb + cb)),
                pl.BlockSpec((K, tc), lambda cb, i: (0, wb + cb))]
    out_shape = [SDS((T, width), dx_dtype), SDS((K, width), f32), SDS((1, width), f32)]
    out_specs = [pl.BlockSpec((tt, tc), lambda cb, i: (i, cb)), pl.BlockSpec((K, tc), lambda cb, i: (0, cb)),
                 pl.BlockSpec((1, tc), lambda cb, i: (0, cb))]
    return _pcall(body, out_shape=out_shape, grid=(width // tc, nt), in_specs=in_specs, out_specs=out_specs,
                  scratch_shapes=[pltpu.VMEM((tt + HALO, tc), f32), pltpu.VMEM((tt + HALO, tc), f32),
                                  pltpu.VMEM((K, 8, tc), f32), pltpu.VMEM((8, tc), f32)],
                  compiler_params=_cparams(("parallel", "arbitrary")), name=name)(dy, dy, x, x, w)


def _mm_b(eq, a, b):
    return jnp.einsum(eq, a.astype(bf16), b.astype(bf16), preferred_element_type=f32)


def _split_bf16(a):
    hi = a.astype(bf16)
    return hi, (a - hi.astype(f32)).astype(bf16)


def _mm_3(eq, a, b):
    ah, al = _split_bf16(a)
    bh, bl = _split_bf16(b)
    e = lambda x, y: jnp.einsum(eq, x, y, preferred_element_type=f32)
    return e(ah, bh) + (e(ah, bl) + e(al, bh))


def _unit_lower_inverse(L):
    C = L.shape[-1]
    ii = lax.broadcasted_iota(jnp.int32, (C, C), 0)
    jj = lax.broadcasted_iota(jnp.int32, (C, C), 1)
    P = -L
    inv = (ii == jj).astype(f32)[None] + P
    span = 2
    while span < C:
        P = _mm_3("hij,hjk->hik", P, P)
        inv = inv + _mm_3("hij,hjk->hik", inv, P)
        span *= 2
    return inv


@jax.custom_vjp
def _known_inverse(L, inv):
    return inv


def _known_inverse_fwd(L, inv):
    return inv, inv


def _known_inverse_bwd(inv, g):
    t = _mm_3("hji,hjk->hik", inv, g)
    return -_mm_3("hij,hkj->hik", t, inv), jnp.zeros_like(inv)


_known_inverse.defvjp(_known_inverse_fwd, _known_inverse_bwd)


def _gdn_chunk(n_heads, head0, inv_known, S, qr, kr, vr, zg, zs, alog, dtb, nw):
    HB, C, dk = qr.shape
    q = _silu(qr)
    k = _silu(kr)
    v = _silu(vr)
    q = q * lax.rsqrt(jnp.sum(q * q, axis=-1, keepdims=True) + 1e-6) * (dk ** -0.5)
    k = k * lax.rsqrt(jnp.sum(k * k, axis=-1, keepdims=True) + 1e-6)
    beta_all = _sigmoid(zs)
    la_all = -jnp.exp(alog) * _softplus(zs + dtb)
    lane = lax.broadcasted_iota(jnp.int32, (C, LANES), 1)
    betas, las = [], []
    for h in range(HB):
        betas.append(jnp.sum(jnp.where(lane == head0 + h, beta_all, 0.0), axis=-1, keepdims=True))
        las.append(jnp.sum(jnp.where(lane == n_heads + head0 + h, la_all, 0.0), axis=-1, keepdims=True))
    beta = jnp.stack(betas, axis=0)
    la = jnp.stack(las, axis=0)
    ii = lax.broadcasted_iota(jnp.int32, (C, C), 0)
    jj = lax.broadcasted_iota(jnp.int32, (C, C), 1)
    eye = (ii == jj).astype(f32)[None]
    causal = (jj <= ii)[None]
    strict = (jj < ii)[None]
    la_row = jnp.sum(la * eye, axis=1, keepdims=True)
    g_col = jnp.sum(jnp.where(causal, la_row, 0.0), axis=2, keepdims=True)
    g_row = jnp.sum(jnp.where((ii <= jj)[None], la, 0.0), axis=1, keepdims=True)
    g_last = jnp.sum(la, axis=1, keepdims=True)
    decay = jnp.where(causal, jnp.exp(jnp.where(causal, g_col - g_row, 0.0)), 0.0)
    kb = k * beta
    L = jnp.where(strict, _mm_b("hid,hjd->hij", kb, k) * decay, 0.0)
    inv = _unit_lower_inverse(L) if inv_known is None else _known_inverse(L, inv_known)
    eg = jnp.exp(g_col)
    u = _mm_b("hij,hjd->hid", inv, v * beta)
    w = _mm_b("hij,hjd->hid", inv, kb * eg)
    attn = jnp.where(causal, _mm_b("hid,hjd->hij", q, k) * decay, 0.0)
    qd = q * eg
    kd = k * jnp.exp(g_last - g_col)
    v_new = u - _mm_b("hck,hkv->hcv", w, S)
    o = _mm_b("hck,hkv->hcv", qd, S) + _mm_b("hcj,hjv->hcv", attn, v_new)
    S_new = S * jnp.exp(g_last) + _mm_b("hck,hcv->hkv", kd, v_new)
    o = o * lax.rsqrt(jnp.mean(o * o, axis=-1, keepdims=True) + 1e-6) * nw * _silu(zg)
    return S_new, o, inv


def _split_heads(x, HB):
    return jnp.stack([x[:, h * HEAD_DIM:(h + 1) * HEAD_DIM] for h in range(HB)], axis=0)


def _merge_heads(x):
    return jnp.concatenate([x[h] for h in range(x.shape[0])], axis=-1)


def _gdn_dims(T, D):
    H = D // HEAD_DIM
    HB = min(8, H)
    tt = min(256, T)
    return H, HB, tt, tt // CHUNK


def gdn_fwd(qkv, z, zg_col0, zs_col0, alog_vec, dtb_vec, nw, name):
    T, D3 = qkv.shape
    D = D3 // 3
    H, HB, tt, nc = _gdn_dims(T, D)
    W = HB * HEAD_DIM
    nhg = H // HB
    zgb, zsb = zg_col0 // W, zs_col0 // LANES

    def body(q_ref, k_ref, v_ref, zg_ref, zs_ref, alog_ref, dtb_ref, nw_ref, o_ref, ssave_ref, isave_ref, s_ref):
        i, hg = pl.program_id(0), pl.program_id(1)
        hsl = pl.ds(hg * HB, HB)

        @pl.when(i == 0)
        def _():
            s_ref[hsl] = jnp.zeros((HB, HEAD_DIM, HEAD_DIM), f32)

        alog, dtb, nwv = alog_ref[...], dtb_ref[...], nw_ref[...]

        def step(c, carry):
            rows = pl.ds(pl.multiple_of(c * CHUNK, CHUNK), CHUNK)
            S = s_ref[hsl]
            ssave_ref[:, pl.ds(c, 1)] = S[:, None]
            S_new, o, inv = _gdn_chunk(H, hg * HB, None, S, _split_heads(q_ref[rows, :], HB),
                                       _split_heads(k_ref[rows, :], HB), _split_heads(v_ref[rows, :], HB),
                                       _split_heads(zg_ref[rows, :], HB), zs_ref[rows, :], alog, dtb, nwv)
            s_ref[hsl] = S_new
            isave_ref[:, pl.ds(c, 1)] = inv[:, None]
            o_ref[rows, :] = _merge_heads(o).astype(bf16)
            return carry

        lax.fori_loop(0, nc, step, 0)

    col = lambda off: pl.BlockSpec((tt, W), lambda i, hg, off=off: (i, off + hg))
    in_specs = [col(0), col(nhg), col(2 * nhg), col(zgb), pl.BlockSpec((tt, LANES), lambda i, hg: (i, zsb)),
                pl.BlockSpec((1, LANES), lambda i, hg: (0, 0)), pl.BlockSpec((1, LANES), lambda i, hg: (0, 0)),
                pl.BlockSpec((1, LANES), lambda i, hg: (0, 0))]
    out_shape = [SDS((T, D), bf16), SDS((H, T // CHUNK, HEAD_DIM, HEAD_DIM), f32), SDS((H, T // CHUNK, CHUNK, CHUNK), f32)]
    out_specs = [pl.BlockSpec((tt, W), lambda i, hg: (i, hg)),
                 pl.BlockSpec((HB, nc, HEAD_DIM, HEAD_DIM), lambda i, hg: (hg, i, 0, 0)),
                 pl.BlockSpec((HB, nc, CHUNK, CHUNK), lambda i, hg: (hg, i, 0, 0))]
    return _pcall(body, out_shape=out_shape, grid=(T // tt, nhg), in_specs=in_specs, out_specs=out_specs,
                  scratch_shapes=[pltpu.VMEM((H, HEAD_DIM, HEAD_DIM), f32)],
                  compiler_params=_cparams(("arbitrary", "arbitrary")), name=name)(qkv, qkv, qkv, z, z, alog_vec, dtb_vec, nw)


def gdn_bwd(do, qkv, z, zg_col0, zs_col0, alog_vec, dtb_vec, nw, ssave, isave, name):
    T, D3 = qkv.shape
    D = D3 // 3
    H, HB, tt, nc = _gdn_dims(T, D)
    W = HB * HEAD_DIM
    nhg = H // HB
    nt = T // tt
    zgb, zsb = zg_col0 // W, zs_col0 // LANES

    def body(do_ref, q_ref, k_ref, v_ref, zg_ref, zs_ref, alog_ref, dtb_ref, nw_ref, ssave_ref, isave_ref,
             dq_ref, dk_ref, dv_ref, dzg_ref, dzs_ref, dalog_ref, ddtb_ref, dnw_ref, ds_ref):
        i, hg = pl.program_id(0), pl.program_id(1)
        hsl = pl.ds(hg * HB, HB)

        @pl.when(i == 0)
        def _():
            ds_ref[hsl] = jnp.zeros((HB, HEAD_DIM, HEAD_DIM), f32)

        @pl.when(jnp.logical_and(i == 0, hg == 0))
        def _():
            dalog_ref[...] = jnp.zeros_like(dalog_ref)
            ddtb_ref[...] = jnp.zeros_like(ddtb_ref)
            dnw_ref[...] = jnp.zeros_like(dnw_ref)

        @pl.when(hg == 0)
        def _():
            dzs_ref[...] = jnp.zeros_like(dzs_ref)

        alog, dtb, nwv = alog_ref[...], dtb_ref[...], nw_ref[...]

        def step(cc, carry):
            c = nc - 1 - cc
            rows = pl.ds(pl.multiple_of(c * CHUNK, CHUNK), CHUNK)
            S = ssave_ref[:, pl.ds(c, 1)][:, 0]
            inv = isave_ref[:, pl.ds(c, 1)][:, 0]

            def fn(*xs):
                S_new, o, _ = _gdn_chunk(H, hg * HB, inv, *xs)
                return S_new, o

            _, pull = jax.vjp(fn, S, _split_heads(q_ref[rows, :], HB), _split_heads(k_ref[rows, :], HB),
                              _split_heads(v_ref[rows, :], HB), _split_heads(zg_ref[rows, :], HB),
                              zs_ref[rows, :], alog, dtb, nwv)
            dS, dq, dk, dv, dzg, dzs, dal, ddt, dnw = pull((ds_ref[hsl], _split_heads(do_ref[rows, :], HB)))
            ds_ref[hsl] = dS
            dq_ref[rows, :] = _merge_heads(dq)
            dk_ref[rows, :] = _merge_heads(dk)
            dv_ref[rows, :] = _merge_heads(dv)
            dzg_ref[rows, :] = _merge_heads(dzg).astype(bf16)
            dzs_ref[rows, :] += dzs
            dalog_ref[...] += dal
            ddtb_ref[...] += ddt
            dnw_ref[...] += dnw
            return carry

        lax.fori_loop(0, nc, step, 0)

    rev = lambda i: nt - 1 - i
    col = lambda off: pl.BlockSpec((tt, W), lambda i, hg, off=off: (rev(i), off + hg))
    vec = lambda r: pl.BlockSpec((r, LANES), lambda i, hg: (0, 0))
    in_specs = [col(0), col(0), col(nhg), col(2 * nhg), col(zgb), pl.BlockSpec((tt, LANES), lambda i, hg: (rev(i), zsb)),
                vec(1), vec(1), vec(1),
                pl.BlockSpec((HB, nc, HEAD_DIM, HEAD_DIM), lambda i, hg: (hg, rev(i), 0, 0)),
                pl.BlockSpec((HB, nc, CHUNK, CHUNK), lambda i, hg: (hg, rev(i), 0, 0))]
    out_shape = [SDS((T, D), f32), SDS((T, D), f32), SDS((T, D), f32), SDS((T, D), bf16), SDS((T, LANES), f32),
                 SDS((1, LANES), f32), SDS((1, LANES), f32), SDS((1, LANES), f32)]
    out_specs = [col(0), col(0), col(0), col(0), pl.BlockSpec((tt, LANES), lambda i, hg: (rev(i), 0)),
                 vec(1), vec(1), vec(1)]
    return _pcall(body, out_shape=out_shape, grid=(nt, nhg), in_specs=in_specs, out_specs=out_specs,
                  scratch_shapes=[pltpu.VMEM((H, HEAD_DIM, HEAD_DIM), f32)],
                  compiler_params=_cparams(("arbitrary", "arbitrary")), name=name)(
        do, qkv, qkv, qkv, z, z, alog_vec, dtb_vec, nw, ssave, isave)


def colsum(a, name, tt=512):
    T, N = a.shape
    tt = min(tt, T)
    tn = _pick(N, 2048, LANES)

    def body(a_ref, o_ref):
        i = pl.program_id(1)
        s = jnp.sum(a_ref[...].astype(f32), axis=0, keepdims=True)

        @pl.when(i == 0)
        def _():
            o_ref[...] = s

        @pl.when(i > 0)
        def _():
            o_ref[...] += s

    return _pcall(body, out_shape=SDS((1, N), f32), grid=(N // tn, T // tt),
                  in_specs=[pl.BlockSpec((tt, tn), lambda j, i: (i, j))], out_specs=pl.BlockSpec((1, tn), lambda j, i: (0, j)),
                  compiler_params=_cparams(("parallel", "arbitrary")), name=name)(a)


def loss_and_grad(y, target, name, tt=256):
    T, D = y.shape
    tt = min(tt, T)

    def body(y_ref, t_ref, loss_ref, dy_ref):
        i = pl.program_id(0)
        e = y_ref[...] - t_ref[...]
        dy_ref[...] = e * (1.0 / D)
        part = jnp.sum(jnp.sum(e * e, axis=1, keepdims=True), axis=0, keepdims=True) * (0.5 / D)
        part = jnp.broadcast_to(part, (1, LANES))

        @pl.when(i == 0)
        def _():
            loss_ref[...] = part

        @pl.when(i > 0)
        def _():
            loss_ref[...] += part

    blk = pl.BlockSpec((tt, D), lambda i: (i, 0))
    return _pcall(body, out_shape=[SDS((1, LANES), f32), SDS((T, D), f32)], grid=(T // tt,), in_specs=[blk, blk],
                  out_specs=[pl.BlockSpec((1, LANES), lambda i: (0, 0)), blk],
                  compiler_params=_cparams(("arbitrary",)), name=name)(y, target)


def adamw(w, g, m, v, name):
    L, R, C = w.shape
    tr = R if R % 8 else _pick(R, max(8, (1 << 19) // C // 8 * 8), 8)
    c1 = 1.0 / (1.0 - ADAM_B1 ** ADAM_STEP)
    c2 = 1.0 / (1.0 - ADAM_B2 ** ADAM_STEP)

    def body(w_ref, g_ref, m_ref, v_ref, d_ref, nm_ref, nv_ref):
        gg = g_ref[...]
        nm = ADAM_B1 * m_ref[...] + (1.0 - ADAM_B1) * gg
        nv = ADAM_B2 * v_ref[...] + (1.0 - ADAM_B2) * (gg * gg)
        m_hat = nm * c1
        v_hat = nv * c2
        d_ref[...] = -ADAM_LR * (m_hat / (jnp.sqrt(v_hat) + ADAM_EPS) + ADAM_WD * w_ref[...])
        nm_ref[...] = nm
        nv_ref[...] = nv

    blk = pl.BlockSpec((1, tr, C), lambda l, r: (l, r, 0))
    shp = SDS((L, R, C), f32)
    return _pcall(body, out_shape=[shp, shp, shp], grid=(L, R // tr), in_specs=[blk] * 4, out_specs=[blk] * 3,
                  compiler_params=_cparams(("parallel", "parallel")), name=name)(w, g, m, v)


HBM_SPEC = pl.BlockSpec(memory_space=pltpu.HBM)


def _dma_sems(n):
    return pltpu.SemaphoreType.DMA((n,))


def all_gather(shards, name):
    n = len(shards)

    def body(*refs):
        x_refs, out_refs = refs[:n], refs[n:2 * n]
        send_sems, recv_sems, local_sems = refs[2 * n:]
        x, y, c = lax.axis_index("x"), lax.axis_index("y"), lax.axis_index("c")
        me, sibling = (x, y, c), (x, y, 1 - c)
        chips = [(1 - x, y), (x, 1 - y), (1 - x, 1 - y)]

        def slot(t, px, py, pc):
            return out_refs[t].at[4 * px + 2 * py + pc]

        def copy(t, k, block, to, src=None):
            return pltpu.make_async_remote_copy(src_ref=slot(t, *block) if src is None else src, dst_ref=slot(t, *block),
                                                send_sem=send_sems.at[7 * t + k], recv_sem=recv_sems.at[7 * t + k],
                                                device_id=to, device_id_type=MESH)

        mine = [pltpu.make_async_copy(x_refs[t], slot(t, *me), local_sems.at[t]) for t in range(n)]
        for cp in mine:
            cp.start()
        first = []
        for t in range(n):
            first.append(copy(t, 0, me, sibling, src=x_refs[t]))
            first += [copy(t, 1 + j, me, (*chip, c), src=x_refs[t]) for j, chip in enumerate(chips)]
        for cp in first:
            cp.start()
        passed = []
        for j, chip in enumerate(chips):
            for t in range(n):
                copy(t, 1 + j, (*chip, c), me).wait_recv()
                cp = copy(t, 4 + j, (*chip, c), sibling)
                cp.start()
                passed.append(cp)
        for t in range(n):
            copy(t, 0, sibling, me).wait_recv()
        for j, chip in enumerate(chips):
            for t in range(n):
                copy(t, 4 + j, (*chip, 1 - c), me).wait_recv()
        for cp in first + passed:
            cp.wait_send()
        for cp in mine:
            cp.wait()

    return _pcall(body, out_shape=[SDS((N_DEV,) + s.shape, s.dtype) for s in shards], in_specs=[HBM_SPEC] * n,
                  out_specs=[HBM_SPEC] * n, scratch_shapes=[_dma_sems(7 * n), _dma_sems(7 * n), _dma_sems(n)],
                  name=name)(*shards)


def exchange_sibling(parts, name):
    n = len(parts)

    def body(*refs):
        p_refs, r_refs = refs[:n], refs[n:2 * n]
        send_sems, recv_sems = refs[2 * n:]
        x, y, c = lax.axis_index("x"), lax.axis_index("y"), lax.axis_index("c")
        cps = []
        for t in range(n):
            for k in range(4):
                cps.append(pltpu.make_async_remote_copy(src_ref=p_refs[t].at[2 * k + 1 - c], dst_ref=r_refs[t].at[k],
                                                        send_sem=send_sems.at[4 * t + k], recv_sem=recv_sems.at[4 * t + k],
                                                        device_id=(x, y, 1 - c), device_id_type=MESH))
        for cp in cps:
            cp.start()
        for cp in cps:
            cp.wait_recv()
        for cp in cps:
            cp.wait_send()

    return _pcall(body, out_shape=[SDS((4,) + p.shape[1:], p.dtype) for p in parts], in_specs=[HBM_SPEC] * n,
                  out_specs=[HBM_SPEC] * n, scratch_shapes=[_dma_sems(4 * n), _dma_sems(4 * n)], name=name)(*parts)


def exchange_chips(s1s, name):
    n = len(s1s)

    def body(*refs):
        s_refs, r_refs = refs[:n], refs[n:2 * n]
        send_sems, recv_sems, local_sems = refs[2 * n:]
        x, y, c = lax.axis_index("x"), lax.axis_index("y"), lax.axis_index("c")
        my_chip = 2 * x + y
        chips = [(1 - x, y), (x, 1 - y), (1 - x, 1 - y)]
        mine = [pltpu.make_async_copy(s_refs[t].at[my_chip], r_refs[t].at[my_chip], local_sems.at[t]) for t in range(n)]
        for cp in mine:
            cp.start()
        cps = []
        for t in range(n):
            for j, (px, py) in enumerate(chips):
                cps.append(pltpu.make_async_remote_copy(src_ref=s_refs[t].at[2 * px + py], dst_ref=r_refs[t].at[my_chip],
                                                        send_sem=send_sems.at[3 * t + j], recv_sem=recv_sems.at[3 * t + j],
                                                        device_id=(px, py, c), device_id_type=MESH))
        for cp in cps:
            cp.start()
        for t in range(n):
            for j, (px, py) in enumerate(chips):
                pltpu.make_async_remote_copy(src_ref=s_refs[t].at[my_chip], dst_ref=r_refs[t].at[2 * px + py],
                                             send_sem=send_sems.at[3 * t + j], recv_sem=recv_sems.at[3 * t + j],
                                             device_id=(px, py, c), device_id_type=MESH).wait_recv()
        for cp in cps:
            cp.wait_send()
        for cp in mine:
            cp.wait()

    return _pcall(body, out_shape=[SDS(s.shape, s.dtype) for s in s1s], in_specs=[HBM_SPEC] * n, out_specs=[HBM_SPEC] * n,
                  scratch_shapes=[_dma_sems(3 * n), _dma_sems(3 * n), _dma_sems(n)], name=name)(*s1s)


def _handshake(peers):
    barrier = pltpu.get_barrier_semaphore()
    for peer in peers:
        pl.semaphore_signal(barrier, inc=1, device_id=peer, device_id_type=MESH)
    pl.semaphore_wait(barrier, len(peers))


def _hbm_ref(a):
    return jax.new_ref(a, memory_space=pltpu.MemorySpace.HBM)


def _hbm_empty(shape, dtype):
    return jax.empty_ref(SDS(shape, dtype), memory_space=pltpu.MemorySpace.HBM)


def all_gather_behind(shards, name, collective_id):
    n = len(shards)
    x_refs = [_hbm_ref(s) for s in shards]
    out_refs = [_hbm_empty((N_DEV,) + s.shape, s.dtype) for s in shards]

    @pl.kernel(mesh=plsc.ScalarSubcoreMesh(axis_name="sequencer", num_cores=1), name=name,
               scratch_types=(_dma_sems(7 * n), _dma_sems(7 * n), _dma_sems(n)),
               compiler_params=pltpu.CompilerParams(collective_id=collective_id))
    def launch(send_sems, recv_sems, local_sems):
        x, y, c = lax.axis_index("x"), lax.axis_index("y"), lax.axis_index("c")
        me, sibling = (x, y, c), (x, y, 1 - c)
        chips = [(1 - x, y), (x, 1 - y), (1 - x, 1 - y)]
        _handshake([sibling] + [(*chip, c) for chip in chips])

        def slot(t, px, py, pc):
            return out_refs[t].at[4 * px + 2 * py + pc]

        def copy(t, k, block, to, src=None):
            return pltpu.make_async_remote_copy(src_ref=slot(t, *block) if src is None else src, dst_ref=slot(t, *block),
                                                send_sem=send_sems.at[7 * t + k], recv_sem=recv_sems.at[7 * t + k],
                                                device_id=to, device_id_type=MESH)

        mine = [pltpu.make_async_copy(x_refs[t], slot(t, *me), local_sems.at[t]) for t in range(n)]
        for cp in mine:
            cp.start()
        first = []
        for t in range(n):
            first.append(copy(t, 0, me, sibling, src=x_refs[t]))
            first += [copy(t, 1 + j, me, (*chip, c), src=x_refs[t]) for j, chip in enumerate(chips)]
        for cp in first:
            cp.start()
        passed = []
        for j, chip in enumerate(chips):
            for t in range(n):
                copy(t, 1 + j, (*chip, c), me).wait_recv()
                cp = copy(t, 4 + j, (*chip, c), sibling)
                cp.start()
                passed.append(cp)
        for t in range(n):
            copy(t, 0, sibling, me).wait_recv()
        for j, chip in enumerate(chips):
            for t in range(n):
                copy(t, 4 + j, (*chip, 1 - c), me).wait_recv()
        for cp in first + passed:
            cp.wait_send()
        for cp in mine:
            cp.wait()

    launch()
    return out_refs


def exchange_chips_behind(s1s, name, collective_id):
    n = len(s1s)
    s_refs = [_hbm_ref(s) for s in s1s]
    r_refs = [_hbm_empty(s.shape, s.dtype) for s in s1s]

    @pl.kernel(mesh=plsc.ScalarSubcoreMesh(axis_name="sequencer", num_cores=1), name=name,
               scratch_types=(_dma_sems(3 * n), _dma_sems(3 * n), _dma_sems(n)),
               compiler_params=pltpu.CompilerParams(collective_id=collective_id))
    def launch(send_sems, recv_sems, local_sems):
        x, y, c = lax.axis_index("x"), lax.axis_index("y"), lax.axis_index("c")
        my_chip = 2 * x + y
        chips = [(1 - x, y), (x, 1 - y), (1 - x, 1 - y)]
        _handshake([(*chip, c) for chip in chips])
        mine = [pltpu.make_async_copy(s_refs[t].at[my_chip], r_refs[t].at[my_chip], local_sems.at[t]) for t in range(n)]
        for cp in mine:
            cp.start()
        cps = []
        for t in range(n):
            for j, (px, py) in enumerate(chips):
                cps.append(pltpu.make_async_remote_copy(src_ref=s_refs[t].at[2 * px + py], dst_ref=r_refs[t].at[my_chip],
                                                        send_sem=send_sems.at[3 * t + j], recv_sem=recv_sems.at[3 * t + j],
                                                        device_id=(px, py, c), device_id_type=MESH))
        for cp in cps:
            cp.start()
        for t in range(n):
            for j, (px, py) in enumerate(chips):
                pltpu.make_async_remote_copy(src_ref=s_refs[t].at[my_chip], dst_ref=r_refs[t].at[2 * px + py],
                                             send_sem=send_sems.at[3 * t + j], recv_sem=recv_sems.at[3 * t + j],
                                             device_id=(px, py, c), device_id_type=MESH).wait_recv()
        for cp in cps:
            cp.wait_send()
        for cp in mine:
            cp.wait()

    launch()
    return r_refs


def _rows_tile(r, c, itemsize):
    return _pick(r, max(16, (1 << 20) // (c * itemsize) // 16 * 16), 16)


def pair_add(part, got, name):
    _, r, c = part.shape
    tr = _rows_tile(r, c, 2)
    core = lax.axis_index("c").reshape(1).astype(jnp.int32)

    def body(core_ref, p_ref, g_ref, o_ref):
        o_ref[...] = (p_ref[...].astype(f32) + g_ref[...].astype(f32)).astype(bf16)

    blk = pl.BlockSpec((None, tr, c), lambda k, i, core_ref: (k, i, 0))
    gs = pltpu.PrefetchScalarGridSpec(
        num_scalar_prefetch=1, grid=(4, r // tr),
        in_specs=[pl.BlockSpec((None, None, tr, c), lambda k, i, core_ref: (k, core_ref[0], i, 0)), blk], out_specs=blk)
    return _pcall(body, out_shape=SDS((4, r, c), bf16), grid_spec=gs, compiler_params=_cparams(("parallel", "parallel")),
                  name=name)(core, part.reshape(4, 2, r, c), got)


def slot_sum(a, name):
    S, r, c = a.shape
    tr = _rows_tile(r, c, a.dtype.itemsize * S)

    def body(a_ref, o_ref):
        acc = a_ref[0].astype(f32)
        for s in range(1, S):
            acc = acc + a_ref[s].astype(f32)
        o_ref[...] = acc

    return _pcall(body, out_shape=SDS((r, c), f32), grid=(r // tr,), in_specs=[pl.BlockSpec((S, tr, c), lambda i: (0, i, 0))],
                  out_specs=pl.BlockSpec((tr, c), lambda i: (i, 0)), compiler_params=_cparams(("parallel",)), name=name)(a)


def reduce_scatter_start(parts, name, collective_id):
    got = exchange_sibling(parts, name + "_c")
    s1 = [pair_add(p, g, f"{name}_add{t}") for t, (p, g) in enumerate(zip(parts, got))]
    return exchange_chips_behind(s1, name + "_xy", collective_id)


def reduce_scatter_finish(refs, name):
    return [slot_sum(r[...], f"{name}_sum{t}") for t, r in enumerate(refs)]


def _to_pack(flat, dtype):
    n = flat.shape[-1]
    unit = 16 * PACK_COLS
    padded = -(-n // unit) * unit
    return jnp.pad(flat.astype(dtype), (0, padded - n)).reshape(padded // PACK_COLS, PACK_COLS)


BIG = ("w_in", "w_conv_proj", "w_gdn_proj", "w_out", "w_ffn_in", "w_ffn_out")
COL_SHARDED = ("w_in", "w_ffn_in", "conv_dw_w", "short_conv_w")
SMALL = ("b_in", "conv_dw_b", "conv_ln_g", "conv_ln_b", "b_conv_proj", "a_log", "dt_bias", "gdn_norm_w",
         "ln1_g", "ln1_b", "ln2_g", "ln2_b")
CONVW = ("conv_dw_w", "short_conv_w")
ORDER = ("w_in", "b_in", "conv_dw_w", "conv_dw_b", "conv_ln_g", "conv_ln_b", "w_conv_proj", "b_conv_proj",
         "short_conv_w", "a_log", "dt_bias", "gdn_norm_w", "w_gdn_proj", "w_out", "ln1_g", "ln1_b",
         "w_ffn_in", "w_ffn_out", "ln2_g", "ln2_b")


def _full_from_gathered(g, name):
    if name in COL_SHARDED:
        return jnp.moveaxis(g, 0, 1).reshape(g.shape[1], N_DEV * g.shape[2])
    return g.reshape(N_DEV * g.shape[1], g.shape[2])


def _w_in_perm(w, D, H):
    pad = jnp.zeros(w.shape[:-1] + (2 * LANES - 2 * H,), w.dtype)
    return jnp.concatenate([w[..., :6 * D], w[..., 6 * D + 2 * H:], w[..., 6 * D:6 * D + 2 * H], pad], axis=-1)


def _w_in_unperm(w, D, H):
    return jnp.concatenate([w[..., :6 * D], w[..., 8 * D:8 * D + 2 * H], w[..., 6 * D:8 * D]], axis=-1)


def kernel(x, w_in, b_in, conv_dw_w, conv_dw_b, conv_ln_g, conv_ln_b, w_conv_proj, b_conv_proj, short_conv_w, a_log, dt_bias, gdn_norm_w, w_gdn_proj, w_out, ln1_g, ln1_b, w_ffn_in, w_ffn_out, ln2_g, ln2_b, loss_target, m_w_in, m_b_in, m_conv_dw_w, m_conv_dw_b, m_conv_ln_g, m_conv_ln_b, m_w_conv_proj, m_b_conv_proj, m_short_conv_w, m_a_log, m_dt_bias, m_gdn_norm_w, m_w_gdn_proj, m_w_out, m_ln1_g, m_ln1_b, m_w_ffn_in, m_w_ffn_out, m_ln2_g, m_ln2_b, v_w_in, v_b_in, v_conv_dw_w, v_conv_dw_b, v_conv_ln_g, v_conv_ln_b, v_w_conv_proj, v_b_conv_proj, v_short_conv_w, v_a_log, v_dt_bias, v_gdn_norm_w, v_w_gdn_proj, v_w_out, v_ln1_g, v_ln1_b, v_w_ffn_in, v_w_ffn_out, v_ln2_g, v_ln2_b):
    W = dict(w_in=w_in, b_in=b_in, conv_dw_w=conv_dw_w, conv_dw_b=conv_dw_b, conv_ln_g=conv_ln_g, conv_ln_b=conv_ln_b,
             w_conv_proj=w_conv_proj, b_conv_proj=b_conv_proj, short_conv_w=short_conv_w, a_log=a_log, dt_bias=dt_bias,
             gdn_norm_w=gdn_norm_w, w_gdn_proj=w_gdn_proj, w_out=w_out, ln1_g=ln1_g, ln1_b=ln1_b, w_ffn_in=w_ffn_in,
             w_ffn_out=w_ffn_out, ln2_g=ln2_g, ln2_b=ln2_b)
    MO = dict(w_in=m_w_in, b_in=m_b_in, conv_dw_w=m_conv_dw_w, conv_dw_b=m_conv_dw_b, conv_ln_g=m_conv_ln_g,
              conv_ln_b=m_conv_ln_b, w_conv_proj=m_w_conv_proj, b_conv_proj=m_b_conv_proj, short_conv_w=m_short_conv_w,
              a_log=m_a_log, dt_bias=m_dt_bias, gdn_norm_w=m_gdn_norm_w, w_gdn_proj=m_w_gdn_proj, w_out=m_w_out,
              ln1_g=m_ln1_g, ln1_b=m_ln1_b, w_ffn_in=m_w_ffn_in, w_ffn_out=m_w_ffn_out, ln2_g=m_ln2_g, ln2_b=m_ln2_b)
    VO = dict(w_in=v_w_in, b_in=v_b_in, conv_dw_w=v_conv_dw_w, conv_dw_b=v_conv_dw_b, conv_ln_g=v_conv_ln_g,
              conv_ln_b=v_conv_ln_b, w_conv_proj=v_w_conv_proj, b_conv_proj=v_b_conv_proj, short_conv_w=v_short_conv_w,
              a_log=v_a_log, dt_bias=v_dt_bias, gdn_norm_w=v_gdn_norm_w, w_gdn_proj=v_w_gdn_proj, w_out=v_w_out,
              ln1_g=v_ln1_g, ln1_b=v_ln1_b, w_ffn_in=v_w_ffn_in, w_ffn_out=v_w_ffn_out, ln2_g=v_ln2_g, ln2_b=v_ln2_b)

    _, T, D = x.shape
    DEPTH = w_in.shape[0]
    H = D // HEAD_DIM
    F = w_ffn_out.shape[1] * N_DEV
    alpha = (2.0 * DEPTH) ** 0.25
    lnres_fn = make_lnres_fn(alpha)
    NA = 8 * D + 2 * LANES
    TT = min(256, T)
    dev = 4 * lax.axis_index("x") + 2 * lax.axis_index("y") + lax.axis_index("c")

    cw_sizes = [(n, W[n].shape[1:]) for n in CONVW]
    cw_flat = jnp.concatenate([W[n].reshape(-1) for n in CONVW])
    cw_g = all_gather([_to_pack(cw_flat, f32)], "ag_convw")[0].reshape(N_DEV, -1)
    conv_full, off = {}, 0
    for n in CONVW:
        sz = W[n].size
        blk = cw_g[:, off:off + sz].reshape((N_DEV,) + W[n].shape)
        conv_full[n] = jnp.moveaxis(blk, 0, 2).reshape(W[n].shape[0], W[n].shape[1], -1)
        off += sz

    def gather_layer_start(l):
        return all_gather_behind([W[n][l].astype(bf16) for n in BIG], f"ag_w{l}", l)

    def gather_layer_finish(refs):
        g = {n: r[...] for n, r in zip(BIG, refs)}
        out = {n: g[n].reshape(-1, g[n].shape[2]) for n in BIG if n not in COL_SHARDED}
        out["w_ffn_in"] = g["w_ffn_in"]
        out["w_in"] = _w_in_perm(_full_from_gathered(g["w_in"], "w_in"), D, H)
        return out

    def lane_vec(v, off=0):
        return jnp.pad(v, (off, LANES - off - v.shape[0])).reshape(1, LANES)

    def row(v):
        return v.reshape(1, -1)

    h32 = x.reshape(T, D)
    h16 = h32.astype(bf16)
    saved = []
    gathering = gather_layer_start(0)
    for l in range(DEPTH):
        G = gather_layer_finish(gathering)
        if l + 1 < DEPTH:
            gathering = gather_layer_start(l + 1)
        b_all = _w_in_perm(b_in[l], D, H)
        alog_vec, dtb_vec = lane_vec(a_log[l], H), lane_vec(dt_bias[l], H)
        nw = row(gdn_norm_w[l])
        z = matmul(h16, G["w_in"], "nn", f"l{l}_mm_in", bias=b_all)
        c0, = rowwise(glu_fn, [(z, 0, D), (z, 1, D)], [], [(f32,)], TT, f"l{l}_glu")
        c1 = conv_fwd(c0, 0, conv_full["conv_dw_w"][l], 0, D, CONV_WIDTH, row(conv_dw_b[l]), f"l{l}_conv")
        c3, = rowwise(lnsilu_fn, [(c1, 0, D)], [row(conv_ln_g[l]), row(conv_ln_b[l])], [(bf16,)], TT, f"l{l}_lnsilu")
        yc = matmul(c3, G["w_conv_proj"], "nn", f"l{l}_mm_cp", bias=b_conv_proj[l])
        qkv = conv_fwd(z, 2 * D, conv_full["short_conv_w"][l], 0, 3 * D, SHORT_CONV, None, f"l{l}_sconv")
        og, ssave, isave = gdn_fwd(qkv, z, 5 * D, 8 * D, alog_vec, dtb_vec, nw, f"l{l}_gdn")
        yg = matmul(og, G["w_gdn_proj"], "nn", f"l{l}_mm_gp")
        m, = rowwise(merge_fn, [(z, 6, D), (z, 7, D), (yc, 0, D), (yg, 0, D)], [], [(bf16,)], TT, f"l{l}_merge")
        mix = matmul(m, G["w_out"], "nn", f"l{l}_mm_out")
        x1_32, x1_16 = rowwise(lnres_fn, [(h32, 0, D), (mix, 0, D)], [row(ln1_g[l]), row(ln1_b[l])], [(f32, bf16)], TT, f"l{l}_ln1")
        hf = matmul(x1_16, G["w_ffn_in"], "nn", f"l{l}_mm_fi", b_blocked=True)
        act, = rowwise(swiglu_fn, [(hf, 0, F), (hf, 1, F)], [], [(bf16,)], min(128, T), f"l{l}_swiglu")
        ff = matmul(act, G["w_ffn_out"], "nn", f"l{l}_mm_fo")
        x2_32, x2_16 = rowwise(lnres_fn, [(x1_32, 0, D), (ff, 0, D)], [row(ln2_g[l]), row(ln2_b[l])], [(f32, bf16)], TT, f"l{l}_ln2")
        saved.append(dict(G=G, h32=h32, h16=h16, z=z, c0=c0, c1=c1, c3=c3, yc=yc, qkv=qkv, ssave=ssave, isave=isave, og=og, yg=yg, m=m,
                          mix=mix, x1_32=x1_32, x1_16=x1_16, hf=hf, act=act, ff=ff, alog_vec=alog_vec, dtb_vec=dtb_vec, nw=nw))
        h32, h16 = x2_32, x2_16

    loss_vec, dy = loss_and_grad(h32, loss_target.reshape(T, D), "loss")
    loss = lax.psum(loss_vec[0, 0], AXES)

    gsmall = {n: [None] * DEPTH for n in SMALL + CONVW}
    gbig = {n: [None] * DEPTH for n in BIG}
    dh = [dy]
    reducing = []

    def finish_reduce(l, refs):
        for n, red in zip(BIG, reduce_scatter_finish(refs, f"rs{l}")):
            gbig[n][l] = red

    for l in reversed(range(DEPTH)):
        s = saved[l]
        G = s["G"]
        TB = min(128, T)
        d_x1r, d_ff, dg2, db2 = rowwise_vjp(lnres_fn, [(s["x1_32"], 0, D), (s["ff"], 0, D)], [row(ln2_g[l]), row(ln2_b[l])],
                                            [dh], [f32, bf16], TB, f"l{l}_ln2_b")
        d_act = matmul(d_ff, G["w_ffn_out"], "nt", f"l{l}_mm_fo_dx")
        dw_fo = matmul(s["act"], d_ff, "tn", f"l{l}_mm_fo_dw", tk_cap=1024, out_dtype=bf16)
        d_gate, d_up = rowwise_vjp(swiglu_fn, [(s["hf"], 0, F), (s["hf"], 1, F)], [], [[d_act]], [bf16, bf16], min(64, T), f"l{l}_swiglu_b")
        d_hf = jnp.concatenate([d_gate, d_up], axis=1)
        d_x1m = matmul(d_hf, G["w_ffn_in"], "nt", f"l{l}_mm_fi_dx", b_blocked=True)
        dw_fi = matmul(s["x1_16"], d_hf, "tn", f"l{l}_mm_fi_dw", tk_cap=1024, out_dtype=bf16, out_blocked=True)
        d_hr, d_mix, dg1, db1 = rowwise_vjp(lnres_fn, [(s["h32"], 0, D), (s["mix"], 0, D)], [row(ln1_g[l]), row(ln1_b[l])],
                                            [[d_x1r, d_x1m]], [f32, bf16], TB, f"l{l}_ln1_b")
        d_m = matmul(d_mix, G["w_out"], "nt", f"l{l}_mm_out_dx")
        dw_out = matmul(s["m"], d_mix, "tn", f"l{l}_mm_out_dw", tk_cap=1024, out_dtype=bf16)
        d_ga, d_gb, d_yc, d_yg = rowwise_vjp(merge_fn, [(s["z"], 6, D), (s["z"], 7, D), (s["yc"], 0, D), (s["yg"], 0, D)], [],
                                             [[d_m]], [bf16] * 4, TB, f"l{l}_merge_b")
        d_c3 = matmul(d_yc, G["w_conv_proj"], "nt", f"l{l}_mm_cp_dx")
        dw_cp = matmul(s["c3"], d_yc, "tn", f"l{l}_mm_cp_dw", tk_cap=1024, out_dtype=bf16)
        db_cp = colsum(d_yc, f"l{l}_cs_cp")
        d_c1, dcg, dcb = rowwise_vjp(lnsilu_fn, [(s["c1"], 0, D)], [row(conv_ln_g[l]), row(conv_ln_b[l])], [[d_c3]], [f32], TB, f"l{l}_lnsilu_b")
        d_c0, dw31, db31 = conv_bwd(d_c1, s["c0"], 0, conv_full["conv_dw_w"][l], 0, D, CONV_WIDTH, f32, f"l{l}_conv_b")
        d_glu_a, d_glu_b = rowwise_vjp(glu_fn, [(s["z"], 0, D), (s["z"], 1, D)], [], [[d_c0]], [bf16, bf16], TB, f"l{l}_glu_b")
        d_og = matmul(d_yg, G["w_gdn_proj"], "nt", f"l{l}_mm_gp_dx")
        dw_gp = matmul(s["og"], d_yg, "tn", f"l{l}_mm_gp_dw", tk_cap=1024, out_dtype=bf16)
        dq, dk, dv, d_zg, d_zs, dalog, ddtb, dnw = gdn_bwd(d_og, s["qkv"], s["z"], 5 * D, 8 * D, s["alog_vec"], s["dtb_vec"], s["nw"],
                                                            s["ssave"], s["isave"], f"l{l}_gdn_b")
        dxs, dwss = [], []
        for sec, dsec in enumerate((dq, dk, dv)):
            dxp, dwp, _ = conv_bwd(dsec, s["z"], (2 + sec) * D, conv_full["short_conv_w"][l], sec * D, D, SHORT_CONV, bf16,
                                   f"l{l}_sconv_b{sec}")
            dxs.append(dxp)
            dwss.append(dwp)
        dz = jnp.concatenate([d_glu_a, d_glu_b] + dxs + [d_zg, d_ga, d_gb, d_zs.astype(bf16), jnp.zeros((T, LANES), bf16)], axis=1)
        d_hm = matmul(dz, G["w_in"], "nt", f"l{l}_mm_in_dx")
        dw_in = matmul(s["h16"], dz, "tn", f"l{l}_mm_in_dw", tk_cap=1024, out_dtype=bf16)
        db_all = colsum(dz, f"l{l}_cs_in")
        dh = [d_hr, d_hm]

        gsmall["b_in"][l] = _w_in_unperm(db_all[0], D, H)
        gsmall["conv_dw_b"][l] = db31[0]
        gsmall["conv_ln_g"][l], gsmall["conv_ln_b"][l] = dcg[0], dcb[0]
        gsmall["b_conv_proj"][l] = db_cp[0]
        gsmall["a_log"][l], gsmall["dt_bias"][l] = dalog[0, H:2 * H], ddtb[0, H:2 * H]
        gsmall["gdn_norm_w"][l] = dnw[0]
        gsmall["ln1_g"][l], gsmall["ln1_b"][l], gsmall["ln2_g"][l], gsmall["ln2_b"][l] = dg1[0], db1[0], dg2[0], db2[0]
        gsmall["conv_dw_w"][l] = dw31
        gsmall["short_conv_w"][l] = jnp.concatenate(dwss, axis=1)

        dw_in_u = _w_in_unperm(dw_in, D, H)
        blocks = dict(w_in=jnp.moveaxis(dw_in_u.reshape(D, N_DEV, -1), 1, 0), w_ffn_in=dw_fi)
        for n, dw in (("w_conv_proj", dw_cp), ("w_gdn_proj", dw_gp), ("w_out", dw_out), ("w_ffn_out", dw_fo)):
            blocks[n] = dw.reshape(N_DEV, -1, dw.shape[1])
        reducing.append((l, reduce_scatter_start([blocks[n] for n in BIG], f"rs{l}", DEPTH + l)))
        if len(reducing) > 1:
            finish_reduce(*reducing.pop(0))
    finish_reduce(*reducing.pop(0))

    grad_x, = rowwise(add_fn, [(dh[0], 0, D), (dh[1], 0, D)], [], [(f32,)], TT, "grad_x")
    grad_x = grad_x.reshape(1, T, D)

    small_flat = jnp.concatenate([jnp.stack(gsmall[n]).reshape(-1) for n in SMALL + CONVW])
    sg = all_gather([_to_pack(small_flat, f32)], "ag_small")[0]
    small_tot = slot_sum(sg, "small_sum").reshape(-1)
    grads, off = {}, 0
    for n in SMALL:
        grads[n] = small_tot[off:off + W[n].size].reshape(W[n].shape)
        off += W[n].size
    for n in CONVW:
        L_, K_, c_ = W[n].shape
        fullg = small_tot[off:off + L_ * K_ * c_ * N_DEV].reshape(L_, K_, c_ * N_DEV)
        grads[n] = lax.dynamic_slice_in_dim(fullg, dev * c_, c_, axis=2)
        off += L_ * K_ * c_ * N_DEV
    for n in BIG:
        grads[n] = jnp.stack(gbig[n])

    delta, new_m, new_v = {}, {}, {}
    for n in BIG + CONVW:
        delta[n], new_m[n], new_v[n] = adamw(W[n], grads[n], MO[n], VO[n], f"adamw_{n}")
    pk = lambda d: _to_pack(jnp.concatenate([d[n].reshape(-1) for n in SMALL]), f32)[None]
    ds, ms, vs = adamw(pk(W), pk(grads), pk(MO), pk(VO), "adamw_small")
    off = 0
    for n in SMALL:
        sl = lambda a: a.reshape(-1)[off:off + W[n].size].reshape(W[n].shape)
        delta[n], new_m[n], new_v[n] = sl(ds), sl(ms), sl(vs)
        off += W[n].size

    return (loss, grad_x, *[grads[n] for n in ORDER], *[delta[n] for n in ORDER],
            *[new_m[n] for n in ORDER], *[new_v[n] for n in ORDER])
```

```python
import functools
import math

import jax
import jax.numpy as jnp
from jax import lax
from jax.experimental import pallas as pl
from jax.experimental.pallas import tpu as pltpu
from jax.experimental.pallas import tpu_sc as plsc

f32, bf16 = jnp.float32, jnp.bfloat16
SDS = jax.ShapeDtypeStruct
MESH = pl.DeviceIdType.MESH
AXES = ("x", "y", "c")
N_DEV = 8

CONV_WIDTH = 31
SHORT_CONV = 4
HEAD_DIM = 128
CHUNK = 64
LN_EPS = 1e-5
ADAM_LR, ADAM_B1, ADAM_B2, ADAM_EPS, ADAM_WD, ADAM_STEP = 0.001, 0.9, 0.999, 1e-08, 0.01, 10

LANES = 128
HALO = 32
PACK_COLS = 1024
VMEM_LIMIT = 56 * 1024 * 1024


def _pcall(body, **kw):
    return pl.pallas_call(body, **kw)


def _cparams(sem=None):
    return pltpu.CompilerParams(dimension_semantics=sem, vmem_limit_bytes=VMEM_LIMIT)


def _pick(n, cap, mult):
    if n <= cap:
        return n
    for t in range(cap - cap % mult, 0, -mult):
        if n % t == 0:
            return t
    raise ValueError(f"no tile for {n} under {cap} in steps of {mult}")


MXU_DIM = 256


def _pick_mxu(n, cap, mult):
    if n % MXU_DIM:
        return _pick(n, cap, mult)
    wide, fine = _pick(n, cap, MXU_DIM), _pick(n, cap, max(mult, LANES))
    return fine if 2 * fine >= 3 * wide else wide


def matmul(a, b, mode, name, bias=None, out_dtype=f32, tm_cap=1024, tn_cap=1280, tk_cap=2048,
           b_blocked=False, out_blocked=False):
    if b_blocked:
        nb, br, bc = b.shape
        b2 = (br, nb * bc)
    else:
        b2 = b.shape
    if mode == "nn":
        (M, K), (K2, N) = a.shape, b2
    elif mode == "nt":
        (M, K), (N, K2) = a.shape, b2
    else:
        (K, M), (K2, N) = a.shape, b2
    assert K == K2, (a.shape, b.shape, mode)
    tm = _pick_mxu(M, tm_cap, 8)
    tn = _pick_mxu(N, tn_cap, LANES)
    tk = _pick_mxu(K, tk_cap, LANES)
    if b_blocked and mode == "nn":
        tn = N // N_DEV
    if b_blocked and mode == "nt":
        tk = K // N_DEV
    if out_blocked:
        tn = N // N_DEV
    nm, nn, nk = M // tm, N // tn, K // tk
    dims = {"nn": (((1,), (0,)), ((), ())), "nt": (((1,), (1,)), ((), ())), "tn": (((0,), (0,)), ((), ()))}[mode]
    has_bias = bias is not None

    def body(*refs):
        a_ref, b_ref = refs[0], refs[1]
        bias_ref = refs[2] if has_bias else None
        o_ref = refs[2 + has_bias]
        prod = lax.dot_general(a_ref[...], b_ref[...], dims, preferred_element_type=f32)

        def finish(acc):
            if has_bias:
                acc = acc + bias_ref[...]
            o_ref[...] = acc.astype(out_dtype)

        if nk == 1:
            finish(prod)
        else:
            acc_ref = refs[3 + has_bias]
            k = pl.program_id(2)

            @pl.when(k == 0)
            def _():
                acc_ref[...] = prod

            @pl.when(jnp.logical_and(k > 0, k < nk - 1))
            def _():
                acc_ref[...] += prod

            @pl.when(k == nk - 1)
            def _():
                finish(acc_ref[...] + prod)

    a_bytes, b_bytes = M * K, N * K
    m_outer = a_bytes >= b_bytes
    if m_outer:
        grid = (nm, nn, nk)
        gi = lambda i, j, k: (i, j, k)
    else:
        grid = (nn, nm, nk)
        gi = lambda j, i, k: (i, j, k)

    def amap(*g):
        i, j, k = gi(*g)
        return (k, i) if mode == "tn" else (i, k)

    def bmap(*g):
        i, j, k = gi(*g)
        return (j, k) if mode == "nt" else (k, j)

    def omap(*g):
        i, j, k = gi(*g)
        return (i, j)

    def biasmap(*g):
        i, j, k = gi(*g)
        return (0, j)

    def bmap_blocked(*g):
        i, j, k = gi(*g)
        return (k, j, 0) if mode == "nt" else (j, k, 0)

    def omap_blocked(*g):
        i, j, k = gi(*g)
        return (j, i, 0)

    b_block = (tn, tk) if mode == "nt" else (tk, tn)
    in_specs = [pl.BlockSpec((tk, tm) if mode == "tn" else (tm, tk), amap),
                pl.BlockSpec((None,) + b_block, bmap_blocked) if b_blocked else pl.BlockSpec(b_block, bmap)]
    args = [a, b]
    if has_bias:
        in_specs.append(pl.BlockSpec((1, tn), biasmap))
        args.append(bias.reshape(1, N).astype(f32))
    if out_blocked:
        out_shape, out_spec = SDS((N_DEV, M, tn), out_dtype), pl.BlockSpec((None, tm, tn), omap_blocked)
    else:
        out_shape, out_spec = SDS((M, N), out_dtype), pl.BlockSpec((tm, tn), omap)
    return _pcall(body, out_shape=out_shape, grid=grid, in_specs=in_specs,
                  out_specs=out_spec,
                  scratch_shapes=[pltpu.VMEM((tm, tn), f32)] if nk > 1 else [],
                  compiler_params=_cparams(("parallel", "parallel", "arbitrary")), name=name)(*args)


def _row_specs(rows, tt):
    specs, args = [], []
    for arr, cb, width in rows:
        specs.append(pl.BlockSpec((tt, width), lambda i, cb=cb: (i, cb)))
        args.append(arr)
    return specs, args


def _param_specs(params):
    return [pl.BlockSpec(p.shape, lambda i: (0, 0)) for p in params]


def rowwise(fn, rows, params, out_dtypes, tt, name):
    T = rows[0][0].shape[0]
    nr, npar = len(rows), len(params)
    blocks = [SDS((tt, w), f32) for _, _, w in rows] + [SDS(p.shape, f32) for p in params]
    outs = jax.eval_shape(fn, *blocks)
    out_shape, out_specs = [], []
    for o, dts in zip(outs, out_dtypes):
        for dt in dts:
            out_shape.append(SDS((T, o.shape[1]), dt))
            out_specs.append(pl.BlockSpec((tt, o.shape[1]), lambda i: (i, 0)))

    def body(*refs):
        xs = [r[...].astype(f32) for r in refs[:nr + npar]]
        res = fn(*xs)
        k = nr + npar
        for o, dts in zip(res, out_dtypes):
            for dt in dts:
                refs[k][...] = o.astype(dt)
                k += 1

    rspecs, rargs = _row_specs(rows, tt)
    return _pcall(body, out_shape=out_shape, grid=(T // tt,), in_specs=rspecs + _param_specs(params),
                  out_specs=out_specs, compiler_params=_cparams(("parallel",)), name=name)(*rargs, *params)


def rowwise_vjp(fn, rows, params, cots, d_dtypes, tt, name, pack=False):
    T = rows[0][0].shape[0]
    nr, npar = len(rows), len(params)
    ncot = [len(c) for c in cots]
    flat_cots = [c for cs in cots for c in cs]

    def body(*refs):
        i = pl.program_id(0)
        xs = [r[...].astype(f32) for r in refs[:nr + npar]]
        k = nr + npar
        cs = []
        for n in ncot:
            tot = refs[k][...].astype(f32)
            for r in refs[k + 1:k + n]:
                tot = tot + r[...].astype(f32)
            cs.append(tot)
            k += n
        _, pull = jax.vjp(fn, *xs)
        grads = pull(tuple(cs))
        if pack:
            refs[k][...] = jnp.concatenate([g.astype(d_dtypes[0]) for g in grads[:nr]], axis=1)
            k += 1
        else:
            for g, dt in zip(grads[:nr], d_dtypes):
                refs[k][...] = g.astype(dt)
                k += 1
        for g in grads[nr:]:
            ref = refs[k]
            k += 1

            @pl.when(i == 0)
            def _(ref=ref, g=g):
                ref[...] = g

            @pl.when(i > 0)
            def _(ref=ref, g=g):
                ref[...] += g

    rspecs, rargs = _row_specs(rows, tt)
    cot_specs = [pl.BlockSpec((tt, c.shape[1]), lambda i: (i, 0)) for c in flat_cots]
    widths = [sum(w for _, _, w in rows)] if pack else [w for _, _, w in rows]
    out_shape = [SDS((T, w), dt) for w, dt in zip(widths, d_dtypes)] + [SDS(p.shape, f32) for p in params]
    out_specs = [pl.BlockSpec((tt, w), lambda i: (i, 0)) for w in widths] + _param_specs(params)
    return _pcall(body, out_shape=out_shape, grid=(T // tt,), in_specs=rspecs + _param_specs(params) + cot_specs,
                  out_specs=out_specs, compiler_params=_cparams(("arbitrary",)), name=name)(*rargs, *params, *flat_cots)


def _sigmoid(x):
    return jax.nn.sigmoid(x)


def _silu(x):
    return x * jax.nn.sigmoid(x)


def _softplus(x):
    return jnp.maximum(x, 0.0) + jnp.log(1.0 + jnp.exp(-jnp.abs(x)))


def _layer_norm(x, g, b):
    mu = jnp.mean(x, axis=-1, keepdims=True)
    xc = x - mu
    var = jnp.mean(xc * xc, axis=-1, keepdims=True)
    return xc * lax.rsqrt(var + LN_EPS) * g + b


def glu_fn(a, b):
    return (a * _sigmoid(b),)


def lnsilu_fn(c, g, b):
    return (_silu(_layer_norm(c, g, b)),)


def merge_fn(ga, gb, yc, yg):
    return (_sigmoid(ga) * yc + _sigmoid(gb) * yg,)


def swiglu_fn(gate, up):
    return (_silu(gate) * up,)


def make_lnres_fn(alpha):
    def lnres_fn(h, y, g, b):
        return (_layer_norm(alpha * h + y, g, b),)
    return lnres_fn


def add_fn(a, b):
    return (a + b,)


CONV_ROWS = 64


def _tap_groups(K, first_row):
    groups = {}
    for j in range(K):
        groups.setdefault((first_row + j) % 8, []).append(j)
    out = []
    for taps in groups.values():
        start = first_row + taps[0]
        out.append((start, 8 * (len(taps) - 1), [(j, first_row + j - start) for j in taps]))
    return out


def conv_fwd(x, x_col0, w, w_col0, width, K, bias, name, tt=512, tc=256):
    T = x.shape[0]
    tt = min(tt, T)
    assert width % tc == 0 and x_col0 % tc == 0 and w_col0 % tc == 0 and tt % HALO == 0 and tt % CONV_ROWS == 0
    hb = tt // HALO
    xb, wb = x_col0 // tc, w_col0 // tc
    has_bias = bias is not None
    groups = _tap_groups(K, HALO - (K - 1))

    def body(*refs):
        x_ref, halo_ref, w_ref = refs[:3]
        b_ref = refs[3] if has_bias else None
        o_ref, ext_ref = refs[3 + has_bias], refs[4 + has_bias]
        i = pl.program_id(1)
        ext_ref[pl.ds(0, HALO), :] = jnp.where(i > 0, halo_ref[...], 0.0)
        ext_ref[pl.ds(HALO, tt), :] = x_ref[...]
        for r in range(tt // CONV_ROWS):
            acc = jnp.zeros((CONV_ROWS, tc), f32)
            for start, extra, taps in groups:
                win = ext_ref[pl.ds(r * CONV_ROWS + start, CONV_ROWS + extra), :]
                for j, off in taps:
                    acc = acc + w_ref[j:j + 1, :] * win[off:off + CONV_ROWS]
            if has_bias:
                acc = acc + b_ref[...]
            o_ref[pl.ds(r * CONV_ROWS, CONV_ROWS), :] = acc

    in_specs = [pl.BlockSpec((tt, tc), lambda cb, i: (i, xb + cb)),
                pl.BlockSpec((HALO, tc), lambda cb, i: (jnp.maximum(i * hb - 1, 0), xb + cb)),
                pl.BlockSpec((K, tc), lambda cb, i: (0, wb + cb))]
    args = [x, x, w]
    if has_bias:
        in_specs.append(pl.BlockSpec((1, tc), lambda cb, i: (0, cb)))
        args.append(bias)
    return _pcall(body, out_shape=SDS((T, width), f32), grid=(width // tc, T // tt), in_specs=in_specs,
                  out_specs=pl.BlockSpec((tt, tc), lambda cb, i: (i, cb)),
                  scratch_shapes=[pltpu.VMEM((tt + HALO, tc), f32)],
                  compiler_params=_cparams(("parallel", "arbitrary")), name=name)(*args)


def conv_bwd(dy, x, x_col0, w, w_col0, width, K, dx_dtype, name, tt=512, tc=256):
    T = x.shape[0]
    tt = min(tt, T)
    hb = tt // HALO
    nt = T // tt
    xb, wb = x_col0 // tc, w_col0 // tc
    x_groups = _tap_groups(K, HALO - (K - 1))
    dy_groups = [(start, extra, [(K - 1 - o, off) for o, off in taps]) for start, extra, taps in _tap_groups(K, 0)]
    RC = CONV_ROWS

    def body(dy_ref, dyn_ref, x_ref, halo_ref, w_ref, dx_ref, dw_ref, db_ref, xext_ref, dyext_ref, dwacc_ref, dbacc_ref):
        i = pl.program_id(1)

        @pl.when(i == 0)
        def _():
            dwacc_ref[...] = jnp.zeros_like(dwacc_ref)
            dbacc_ref[...] = jnp.zeros_like(dbacc_ref)

        xext_ref[pl.ds(0, HALO), :] = jnp.where(i > 0, halo_ref[...], 0.0)
        xext_ref[pl.ds(HALO, tt), :] = x_ref[...]
        dyext_ref[pl.ds(0, tt), :] = dy_ref[...].astype(f32)
        dyext_ref[pl.ds(tt, HALO), :] = jnp.where(i < nt - 1, dyn_ref[...].astype(f32), 0.0)

        def fold(p):
            return jnp.sum(p.reshape(RC // 8, 8, tc), axis=0)

        for r in range(tt // RC):
            acc = jnp.zeros((RC, tc), f32)
            for start, extra, taps in dy_groups:
                win = dyext_ref[pl.ds(r * RC + start, RC + extra), :]
                for j, off in taps:
                    acc = acc + w_ref[j:j + 1, :] * win[off:off + RC]
            dx_ref[pl.ds(r * RC, RC), :] = acc.astype(dx_dtype)
            dyc = dyext_ref[pl.ds(r * RC, RC), :]
            dbacc_ref[...] += fold(dyc)
            for start, extra, taps in x_groups:
                win = xext_ref[pl.ds(r * RC + start, RC + extra), :]
                for j, off in taps:
                    dwacc_ref[j] += fold(dyc * win[off:off + RC])

        @pl.when(i == nt - 1)
        def _():
            dw_ref[...] = jnp.sum(dwacc_ref[...], axis=1)
            db_ref[...] = jnp.sum(dbacc_ref[...], axis=0, keepdims=True)

    in_specs = [pl.BlockSpec((tt, tc), lambda cb, i: (i, cb)),
                pl.BlockSpec((HALO, tc), lambda cb, i: (jnp.minimum((i + 1) * hb, nt * hb - 1), cb)),
                pl.BlockSpec((tt, tc), lambda cb, i: (i, xb + cb)),
                pl.BlockSpec((HALO, tc), lambda cb, i: (jnp.maximum(i * hb - 1, 0), xb + cb)),
                pl.BlockSpec((K, tc), lambda cb, i: (0, wb + cb))]
    out_shape = [SDS((T, width), dx_dtype), SDS((K, width), f32), SDS((1, width), f32)]
    out_specs = [pl.BlockSpec((tt, tc), lambda cb, i: (i, cb)), pl.BlockSpec((K, tc), lambda cb, i: (0, cb)),
                 pl.BlockSpec((1, tc), lambda cb, i: (0, cb))]
    return _pcall(body, out_shape=out_shape, grid=(width // tc, nt), in_specs=in_specs, out_specs=out_specs,
                  scratch_shapes=[pltpu.VMEM((tt + HALO, tc), f32), pltpu.VMEM((tt + HALO, tc), f32),
                                  pltpu.VMEM((K, 8, tc), f32), pltpu.VMEM((8, tc), f32)],
                  compiler_params=_cparams(("parallel", "arbitrary")), name=name)(dy, dy, x, x, w)


def _mm_b(eq, a, b):
    return jnp.einsum(eq, a.astype(bf16), b.astype(bf16), preferred_element_type=f32)


def _split_bf16(a):
    hi = a.astype(bf16)
    return hi, (a - hi.astype(f32)).astype(bf16)


def _mm_3(eq, a, b):
    ah, al = _split_bf16(a)
    bh, bl = _split_bf16(b)
    e = lambda x, y: jnp.einsum(eq, x, y, preferred_element_type=f32)
    return e(ah, bh) + (e(ah, bl) + e(al, bh))


def _unit_lower_inverse(L):
    C = L.shape[-1]
    ii = lax.broadcasted_iota(jnp.int32, (C, C), 0)
    jj = lax.broadcasted_iota(jnp.int32, (C, C), 1)
    P = -L
    inv = (ii == jj).astype(f32)[None] + P
    span = 2
    while span < C:
        P = _mm_3("hij,hjk->hik", P, P)
        inv = inv + _mm_3("hij,hjk->hik", inv, P)
        span *= 2
    return inv


@jax.custom_vjp
def _known_inverse(L, inv):
    return inv


def _known_inverse_fwd(L, inv):
    return inv, inv


def _known_inverse_bwd(inv, g):
    t = _mm_3("hji,hjk->hik", inv, g)
    return -_mm_3("hij,hkj->hik", t, inv), jnp.zeros_like(inv)


_known_inverse.defvjp(_known_inverse_fwd, _known_inverse_bwd)


def _gdn_chunk(n_heads, head0, inv_known, S, qr, kr, vr, zg, zs, alog, dtb, nw):
    HB, C, dk = qr.shape
    q = _silu(qr)
    k = _silu(kr)
    v = _silu(vr)
    q = q * lax.rsqrt(jnp.sum(q * q, axis=-1, keepdims=True) + 1e-6) * (dk ** -0.5)
    k = k * lax.rsqrt(jnp.sum(k * k, axis=-1, keepdims=True) + 1e-6)
    beta_all = _sigmoid(zs)
    la_all = -jnp.exp(alog) * _softplus(zs + dtb)
    lane = lax.broadcasted_iota(jnp.int32, (C, LANES), 1)
    betas, las = [], []
    for h in range(HB):
        betas.append(jnp.sum(jnp.where(lane == head0 + h, beta_all, 0.0), axis=-1, keepdims=True))
        las.append(jnp.sum(jnp.where(lane == n_heads + head0 + h, la_all, 0.0), axis=-1, keepdims=True))
    beta = jnp.stack(betas, axis=0)
    la = jnp.stack(las, axis=0)
    ii = lax.broadcasted_iota(jnp.int32, (C, C), 0)
    jj = lax.broadcasted_iota(jnp.int32, (C, C), 1)
    eye = (ii == jj).astype(f32)[None]
    causal = (jj <= ii)[None]
    strict = (jj < ii)[None]
    la_row = jnp.sum(la * eye, axis=1, keepdims=True)
    g_col = jnp.sum(jnp.where(causal, la_row, 0.0), axis=2, keepdims=True)
    g_row = jnp.sum(jnp.where((ii <= jj)[None], la, 0.0), axis=1, keepdims=True)
    g_last = jnp.sum(la, axis=1, keepdims=True)
    decay = jnp.where(causal, jnp.exp(jnp.where(causal, g_col - g_row, 0.0)), 0.0)
    kb = k * beta
    L = jnp.where(strict, _mm_b("hid,hjd->hij", kb, k) * decay, 0.0)
    inv = _unit_lower_inverse(L) if inv_known is None else _known_inverse(L, inv_known)
    eg = jnp.exp(g_col)
    u = _mm_b("hij,hjd->hid", inv, v * beta)
    w = _mm_b("hij,hjd->hid", inv, kb * eg)
    attn = jnp.where(causal, _mm_b("hid,hjd->hij", q, k) * decay, 0.0)
    qd = q * eg
    kd = k * jnp.exp(g_last - g_col)
    v_new = u - _mm_b("hck,hkv->hcv", w, S)
    o = _mm_b("hck,hkv->hcv", qd, S) + _mm_b("hcj,hjv->hcv", attn, v_new)
    S_new = S * jnp.exp(g_last) + _mm_b("hck,hcv->hkv", kd, v_new)
    o = o * lax.rsqrt(jnp.mean(o * o, axis=-1, keepdims=True) + 1e-6) * nw * _silu(zg)
    return S_new, o, inv


def _split_heads(x, HB):
    return jnp.stack([x[:, h * HEAD_DIM:(h + 1) * HEAD_DIM] for h in range(HB)], axis=0)


def _merge_heads(x):
    return jnp.concatenate([x[h] for h in range(x.shape[0])], axis=-1)


def _gdn_dims(T, D):
    H = D // HEAD_DIM
    HB = min(8, H)
    tt = min(256, T)
    return H, HB, tt, tt // CHUNK


def gdn_fwd(qkv, z, zg_col0, zs_col0, alog_vec, dtb_vec, nw, name):
    T, D3 = qkv.shape
    D = D3 // 3
    H, HB, tt, nc = _gdn_dims(T, D)
    W = HB * HEAD_DIM
    nhg = H // HB
    zgb, zsb = zg_col0 // W, zs_col0 // LANES

    def body(q_ref, k_ref, v_ref, zg_ref, zs_ref, alog_ref, dtb_ref, nw_ref, o_ref, ssave_ref, isave_ref, s_ref):
        i, hg = pl.program_id(0), pl.program_id(1)
        hsl = pl.ds(hg * HB, HB)

        @pl.when(i == 0)
        def _():
            s_ref[hsl] = jnp.zeros((HB, HEAD_DIM, HEAD_DIM), f32)

        alog, dtb, nwv = alog_ref[...], dtb_ref[...], nw_ref[...]

        def step(c, carry):
            rows = pl.ds(pl.multiple_of(c * CHUNK, CHUNK), CHUNK)
            S = s_ref[hsl]
            ssave_ref[:, pl.ds(c, 1)] = S[:, None]
            S_new, o, inv = _gdn_chunk(H, hg * HB, None, S, _split_heads(q_ref[rows, :], HB),
                                       _split_heads(k_ref[rows, :], HB), _split_heads(v_ref[rows, :], HB),
                                       _split_heads(zg_ref[rows, :], HB), zs_ref[rows, :], alog, dtb, nwv)
            s_ref[hsl] = S_new
            isave_ref[:, pl.ds(c, 1)] = inv[:, None]
            o_ref[rows, :] = _merge_heads(o).astype(bf16)
            return carry

        lax.fori_loop(0, nc, step, 0)

    col = lambda off: pl.BlockSpec((tt, W), lambda i, hg, off=off: (i, off + hg))
    in_specs = [col(0), col(nhg), col(2 * nhg), col(zgb), pl.BlockSpec((tt, LANES), lambda i, hg: (i, zsb)),
                pl.BlockSpec((1, LANES), lambda i, hg: (0, 0)), pl.BlockSpec((1, LANES), lambda i, hg: (0, 0)),
                pl.BlockSpec((1, LANES), lambda i, hg: (0, 0))]
    out_shape = [SDS((T, D), bf16), SDS((H, T // CHUNK, HEAD_DIM, HEAD_DIM), f32), SDS((H, T // CHUNK, CHUNK, CHUNK), f32)]
    out_specs = [pl.BlockSpec((tt, W), lambda i, hg: (i, hg)),
                 pl.BlockSpec((HB, nc, HEAD_DIM, HEAD_DIM), lambda i, hg: (hg, i, 0, 0)),
                 pl.BlockSpec((HB, nc, CHUNK, CHUNK), lambda i, hg: (hg, i, 0, 0))]
    return _pcall(body, out_shape=out_shape, grid=(T // tt, nhg), in_specs=in_specs, out_specs=out_specs,
                  scratch_shapes=[pltpu.VMEM((H, HEAD_DIM, HEAD_DIM), f32)],
                  compiler_params=_cparams(("arbitrary", "arbitrary")), name=name)(qkv, qkv, qkv, z, z, alog_vec, dtb_vec, nw)


def gdn_bwd(do, qkv, z, zg_col0, zs_col0, alog_vec, dtb_vec, nw, ssave, isave, name):
    T, D3 = qkv.shape
    D = D3 // 3
    H, HB, tt, nc = _gdn_dims(T, D)
    W = HB * HEAD_DIM
    nhg = H // HB
    nt = T // tt
    zgb, zsb = zg_col0 // W, zs_col0 // LANES

    def body(do_ref, q_ref, k_ref, v_ref, zg_ref, zs_ref, alog_ref, dtb_ref, nw_ref, ssave_ref, isave_ref,
             dq_ref, dk_ref, dv_ref, dzg_ref, dzs_ref, dalog_ref, ddtb_ref, dnw_ref, ds_ref):
        i, hg = pl.program_id(0), pl.program_id(1)
        hsl = pl.ds(hg * HB, HB)

        @pl.when(i == 0)
        def _():
            ds_ref[hsl] = jnp.zeros((HB, HEAD_DIM, HEAD_DIM), f32)

        @pl.when(jnp.logical_and(i == 0, hg == 0))
        def _():
            dalog_ref[...] = jnp.zeros_like(dalog_ref)
            ddtb_ref[...] = jnp.zeros_like(ddtb_ref)
            dnw_ref[...] = jnp.zeros_like(dnw_ref)

        @pl.when(hg == 0)
        def _():
            dzs_ref[...] = jnp.zeros_like(dzs_ref)

        alog, dtb, nwv = alog_ref[...], dtb_ref[...], nw_ref[...]

        def step(cc, carry):
            c = nc - 1 - cc
            rows = pl.ds(pl.multiple_of(c * CHUNK, CHUNK), CHUNK)
            S = ssave_ref[:, pl.ds(c, 1)][:, 0]
            inv = isave_ref[:, pl.ds(c, 1)][:, 0]

            def fn(*xs):
                S_new, o, _ = _gdn_chunk(H, hg * HB, inv, *xs)
                return S_new, o

            _, pull = jax.vjp(fn, S, _split_heads(q_ref[rows, :], HB), _split_heads(k_ref[rows, :], HB),
                              _split_heads(v_ref[rows, :], HB), _split_heads(zg_ref[rows, :], HB),
                              zs_ref[rows, :], alog, dtb, nwv)
            dS, dq, dk, dv, dzg, dzs, dal, ddt, dnw = pull((ds_ref[hsl], _split_heads(do_ref[rows, :], HB)))
            ds_ref[hsl] = dS
            dq_ref[rows, :] = _merge_heads(dq)
            dk_ref[rows, :] = _merge_heads(dk)
            dv_ref[rows, :] = _merge_heads(dv)
            dzg_ref[rows, :] = _merge_heads(dzg).astype(bf16)
            dzs_ref[rows, :] += dzs
            dalog_ref[...] += dal
            ddtb_ref[...] += ddt
            dnw_ref[...] += dnw
            return carry

        lax.fori_loop(0, nc, step, 0)

    rev = lambda i: nt - 1 - i
    col = lambda off: pl.BlockSpec((tt, W), lambda i, hg, off=off: (rev(i), off + hg))
    vec = lambda r: pl.BlockSpec((r, LANES), lambda i, hg: (0, 0))
    in_specs = [col(0), col(0), col(nhg), col(2 * nhg), col(zgb), pl.BlockSpec((tt, LANES), lambda i, hg: (rev(i), zsb)),
                vec(1), vec(1), vec(1),
                pl.BlockSpec((HB, nc, HEAD_DIM, HEAD_DIM), lambda i, hg: (hg, rev(i), 0, 0)),
                pl.BlockSpec((HB, nc, CHUNK, CHUNK), lambda i, hg: (hg, rev(i), 0, 0))]
    out_shape = [SDS((T, D), f32), SDS((T, D), f32), SDS((T, D), f32), SDS((T, D), bf16), SDS((T, LANES), f32),
                 SDS((1, LANES), f32), SDS((1, LANES), f32), SDS((1, LANES), f32)]
    out_specs = [col(0), col(0), col(0), col(0), pl.BlockSpec((tt, LANES), lambda i, hg: (rev(i), 0)),
                 vec(1), vec(1), vec(1)]
    return _pcall(body, out_shape=out_shape, grid=(nt, nhg), in_specs=in_specs, out_specs=out_specs,
                  scratch_shapes=[pltpu.VMEM((H, HEAD_DIM, HEAD_DIM), f32)],
                  compiler_params=_cparams(("arbitrary", "arbitrary")), name=name)(
        do, qkv, qkv, qkv, z, z, alog_vec, dtb_vec, nw, ssave, isave)


def colsum(a, name, tt=512):
    T, N = a.shape
    tt = min(tt, T)
    tn = _pick(N, 2048, LANES)

    def body(a_ref, o_ref):
        i = pl.program_id(1)
        s = jnp.sum(a_ref[...].astype(f32), axis=0, keepdims=True)

        @pl.when(i == 0)
        def _():
            o_ref[...] = s

        @pl.when(i > 0)
        def _():
            o_ref[...] += s

    return _pcall(body, out_shape=SDS((1, N), f32), grid=(N // tn, T // tt),
                  in_specs=[pl.BlockSpec((tt, tn), lambda j, i: (i, j))], out_specs=pl.BlockSpec((1, tn), lambda j, i: (0, j)),
                  compiler_params=_cparams(("parallel", "arbitrary")), name=name)(a)


def loss_and_grad(y, target, name, tt=256):
    T, D = y.shape
    tt = min(tt, T)

    def body(y_ref, t_ref, loss_ref, dy_ref):
        i = pl.program_id(0)
        e = y_ref[...] - t_ref[...]
        dy_ref[...] = e * (1.0 / D)
        part = jnp.sum(jnp.sum(e * e, axis=1, keepdims=True), axis=0, keepdims=True) * (0.5 / D)
        part = jnp.broadcast_to(part, (1, LANES))

        @pl.when(i == 0)
        def _():
            loss_ref[...] = part

        @pl.when(i > 0)
        def _():
            loss_ref[...] += part

    blk = pl.BlockSpec((tt, D), lambda i: (i, 0))
    return _pcall(body, out_shape=[SDS((1, LANES), f32), SDS((T, D), f32)], grid=(T // tt,), in_specs=[blk, blk],
                  out_specs=[pl.BlockSpec((1, LANES), lambda i: (0, 0)), blk],
                  compiler_params=_cparams(("arbitrary",)), name=name)(y, target)


def adamw(w, g, m, v, name):
    L, R, C = w.shape
    tr = R if R % 8 else _pick(R, max(8, (1 << 19) // C // 8 * 8), 8)
    c1 = 1.0 / (1.0 - ADAM_B1 ** ADAM_STEP)
    c2 = 1.0 / (1.0 - ADAM_B2 ** ADAM_STEP)

    def body(w_ref, g_ref, m_ref, v_ref, d_ref, nm_ref, nv_ref):
        gg = g_ref[...]
        nm = ADAM_B1 * m_ref[...] + (1.0 - ADAM_B1) * gg
        nv = ADAM_B2 * v_ref[...] + (1.0 - ADAM_B2) * (gg * gg)
        m_hat = nm * c1
        v_hat = nv * c2
        d_ref[...] = -ADAM_LR * (m_hat / (jnp.sqrt(v_hat) + ADAM_EPS) + ADAM_WD * w_ref[...])
        nm_ref[...] = nm
        nv_ref[...] = nv

    blk = pl.BlockSpec((1, tr, C), lambda l, r: (l, r, 0))
    shp = SDS((L, R, C), f32)
    return _pcall(body, out_shape=[shp, shp, shp], grid=(L, R // tr), in_specs=[blk] * 4, out_specs=[blk] * 3,
                  compiler_params=_cparams(("parallel", "parallel")), name=name)(w, g, m, v)


HBM_SPEC = pl.BlockSpec(memory_space=pltpu.HBM)


def _dma_sems(n):
    return pltpu.SemaphoreType.DMA((n,))


def all_gather(shards, name):
    n = len(shards)

    def body(*refs):
        x_refs, out_refs = refs[:n], refs[n:2 * n]
        send_sems, recv_sems, local_sems = refs[2 * n:]
        x, y, c = lax.axis_index("x"), lax.axis_index("y"), lax.axis_index("c")
        me, sibling = (x, y, c), (x, y, 1 - c)
        chips = [(1 - x, y), (x, 1 - y), (1 - x, 1 - y)]

        def slot(t, px, py, pc):
            return out_refs[t].at[4 * px + 2 * py + pc]

        def copy(t, k, block, to, src=None):
            return pltpu.make_async_remote_copy(src_ref=slot(t, *block) if src is None else src, dst_ref=slot(t, *block),
                                                send_sem=send_sems.at[7 * t + k], recv_sem=recv_sems.at[7 * t + k],
                                                device_id=to, device_id_type=MESH)

        mine = [pltpu.make_async_copy(x_refs[t], slot(t, *me), local_sems.at[t]) for t in range(n)]
        for cp in mine:
            cp.start()
        first = []
        for t in range(n):
            first.append(copy(t, 0, me, sibling, src=x_refs[t]))
            first += [copy(t, 1 + j, me, (*chip, c), src=x_refs[t]) for j, chip in enumerate(chips)]
        for cp in first:
            cp.start()
        passed = []
        for j, chip in enumerate(chips):
            for t in range(n):
                copy(t, 1 + j, (*chip, c), me).wait_recv()
                cp = copy(t, 4 + j, (*chip, c), sibling)
                cp.start()
                passed.append(cp)
        for t in range(n):
            copy(t, 0, sibling, me).wait_recv()
        for j, chip in enumerate(chips):
            for t in range(n):
                copy(t, 4 + j, (*chip, 1 - c), me).wait_recv()
        for cp in first + passed:
            cp.wait_send()
        for cp in mine:
            cp.wait()

    return _pcall(body, out_shape=[SDS((N_DEV,) + s.shape, s.dtype) for s in shards], in_specs=[HBM_SPEC] * n,
                  out_specs=[HBM_SPEC] * n, scratch_shapes=[_dma_sems(7 * n), _dma_sems(7 * n), _dma_sems(n)],
                  name=name)(*shards)


def exchange_sibling(parts, name):
    n = len(parts)

    def body(*refs):
        p_refs, r_refs = refs[:n], refs[n:2 * n]
        send_sems, recv_sems = refs[2 * n:]
        x, y, c = lax.axis_index("x"), lax.axis_index("y"), lax.axis_index("c")
        cps = []
        for t in range(n):
            for k in range(4):
                cps.append(pltpu.make_async_remote_copy(src_ref=p_refs[t].at[2 * k + 1 - c], dst_ref=r_refs[t].at[k],
                                                        send_sem=send_sems.at[4 * t + k], recv_sem=recv_sems.at[4 * t + k],
                                                        device_id=(x, y, 1 - c), device_id_type=MESH))
        for cp in cps:
            cp.start()
        for cp in cps:
            cp.wait_recv()
        for cp in cps:
            cp.wait_send()

    return _pcall(body, out_shape=[SDS((4,) + p.shape[1:], p.dtype) for p in parts], in_specs=[HBM_SPEC] * n,
                  out_specs=[HBM_SPEC] * n, scratch_shapes=[_dma_sems(4 * n), _dma_sems(4 * n)], name=name)(*parts)


def exchange_chips(s1s, name):
    n = len(s1s)

    def body(*refs):
        s_refs, r_refs = refs[:n], refs[n:2 * n]
        send_sems, recv_sems, local_sems = refs[2 * n:]
        x, y, c = lax.axis_index("x"), lax.axis_index("y"), lax.axis_index("c")
        my_chip = 2 * x + y
        chips = [(1 - x, y), (x, 1 - y), (1 - x, 1 - y)]
        mine = [pltpu.make_async_copy(s_refs[t].at[my_chip], r_refs[t].at[my_chip], local_sems.at[t]) for t in range(n)]
        for cp in mine:
            cp.start()
        cps = []
        for t in range(n):
            for j, (px, py) in enumerate(chips):
                cps.append(pltpu.make_async_remote_copy(src_ref=s_refs[t].at[2 * px + py], dst_ref=r_refs[t].at[my_chip],
                                                        send_sem=send_sems.at[3 * t + j], recv_sem=recv_sems.at[3 * t + j],
                                                        device_id=(px, py, c), device_id_type=MESH))
        for cp in cps:
            cp.start()
        for t in range(n):
            for j, (px, py) in enumerate(chips):
                pltpu.make_async_remote_copy(src_ref=s_refs[t].at[my_chip], dst_ref=r_refs[t].at[2 * px + py],
                                             send_sem=send_sems.at[3 * t + j], recv_sem=recv_sems.at[3 * t + j],
                                             device_id=(px, py, c), device_id_type=MESH).wait_recv()
        for cp in cps:
            cp.wait_send()
        for cp in mine:
            cp.wait()

    return _pcall(body, out_shape=[SDS(s.shape, s.dtype) for s in s1s], in_specs=[HBM_SPEC] * n, out_specs=[HBM_SPEC] * n,
                  scratch_shapes=[_dma_sems(3 * n), _dma_sems(3 * n), _dma_sems(n)], name=name)(*s1s)


def _handshake(peers):
    barrier = pltpu.get_barrier_semaphore()
    for peer in peers:
        pl.semaphore_signal(barrier, inc=1, device_id=peer, device_id_type=MESH)
    pl.semaphore_wait(barrier, len(peers))


def _hbm_ref(a):
    return jax.new_ref(a, memory_space=pltpu.MemorySpace.HBM)


def _hbm_empty(shape, dtype):
    return jax.empty_ref(SDS(shape, dtype), memory_space=pltpu.MemorySpace.HBM)


def all_gather_behind(shards, name, collective_id):
    n = len(shards)
    x_refs = [_hbm_ref(s) for s in shards]
    out_refs = [_hbm_empty((N_DEV,) + s.shape, s.dtype) for s in shards]

    @pl.kernel(mesh=plsc.ScalarSubcoreMesh(axis_name="sequencer", num_cores=1), name=name,
               scratch_types=(_dma_sems(7 * n), _dma_sems(7 * n), _dma_sems(n)),
               compiler_params=pltpu.CompilerParams(collective_id=collective_id))
    def launch(send_sems, recv_sems, local_sems):
        x, y, c = lax.axis_index("x"), lax.axis_index("y"), lax.axis_index("c")
        me, sibling = (x, y, c), (x, y, 1 - c)
        chips = [(1 - x, y), (x, 1 - y), (1 - x, 1 - y)]
        _handshake([sibling] + [(*chip, c) for chip in chips])

        def slot(t, px, py, pc):
            return out_refs[t].at[4 * px + 2 * py + pc]

        def copy(t, k, block, to, src=None):
            return pltpu.make_async_remote_copy(src_ref=slot(t, *block) if src is None else src, dst_ref=slot(t, *block),
                                                send_sem=send_sems.at[7 * t + k], recv_sem=recv_sems.at[7 * t + k],
                                                device_id=to, device_id_type=MESH)

        mine = [pltpu.make_async_copy(x_refs[t], slot(t, *me), local_sems.at[t]) for t in range(n)]
        for cp in mine:
            cp.start()
        first = []
        for t in range(n):
            first.append(copy(t, 0, me, sibling, src=x_refs[t]))
            first += [copy(t, 1 + j, me, (*chip, c), src=x_refs[t]) for j, chip in enumerate(chips)]
        for cp in first:
            cp.start()
        passed = []
        for j, chip in enumerate(chips):
            for t in range(n):
                copy(t, 1 + j, (*chip, c), me).wait_recv()
                cp = copy(t, 4 + j, (*chip, c), sibling)
                cp.start()
                passed.append(cp)
        for t in range(n):
            copy(t, 0, sibling, me).wait_recv()
        for j, chip in enumerate(chips):
            for t in range(n):
                copy(t, 4 + j, (*chip, 1 - c), me).wait_recv()
        for cp in first + passed:
            cp.wait_send()
        for cp in mine:
            cp.wait()

    launch()
    return out_refs


def exchange_chips_behind(s1s, name, collective_id):
    n = len(s1s)
    s_refs = [_hbm_ref(s) for s in s1s]
    r_refs = [_hbm_empty(s.shape, s.dtype) for s in s1s]

    @pl.kernel(mesh=plsc.ScalarSubcoreMesh(axis_name="sequencer", num_cores=1), name=name,
               scratch_types=(_dma_sems(3 * n), _dma_sems(3 * n), _dma_sems(n)),
               compiler_params=pltpu.CompilerParams(collective_id=collective_id))
    def launch(send_sems, recv_sems, local_sems):
        x, y, c = lax.axis_index("x"), lax.axis_index("y"), lax.axis_index("c")
        my_chip = 2 * x + y
        chips = [(1 - x, y), (x, 1 - y), (1 - x, 1 - y)]
        _handshake([(*chip, c) for chip in chips])
        mine = [pltpu.make_async_copy(s_refs[t].at[my_chip], r_refs[t].at[my_chip], local_sems.at[t]) for t in range(n)]
        for cp in mine:
            cp.start()
        cps = []
        for t in range(n):
            for j, (px, py) in enumerate(chips):
                cps.append(pltpu.make_async_remote_copy(src_ref=s_refs[t].at[2 * px + py], dst_ref=r_refs[t].at[my_chip],
                                                        send_sem=send_sems.at[3 * t + j], recv_sem=recv_sems.at[3 * t + j],
                                                        device_id=(px, py, c), device_id_type=MESH))
        for cp in cps:
            cp.start()
        for t in range(n):
            for j, (px, py) in enumerate(chips):
                pltpu.make_async_remote_copy(src_ref=s_refs[t].at[my_chip], dst_ref=r_refs[t].at[2 * px + py],
                                             send_sem=send_sems.at[3 * t + j], recv_sem=recv_sems.at[3 * t + j],
                                             device_id=(px, py, c), device_id_type=MESH).wait_recv()
        for cp in cps:
            cp.wait_send()
        for cp in mine:
            cp.wait()

    launch()
    return r_refs


def _rows_tile(r, c, itemsize):
    return _pick(r, max(16, (1 << 20) // (c * itemsize) // 16 * 16), 16)


def pair_add(part, got, name):
    _, r, c = part.shape
    tr = _rows_tile(r, c, 2)
    core = lax.axis_index("c").reshape(1).astype(jnp.int32)

    def body(core_ref, p_ref, g_ref, o_ref):
        o_ref[...] = (p_ref[...].astype(f32) + g_ref[...].astype(f32)).astype(bf16)

    blk = pl.BlockSpec((None, tr, c), lambda k, i, core_ref: (k, i, 0))
    gs = pltpu.PrefetchScalarGridSpec(
        num_scalar_prefetch=1, grid=(4, r // tr),
        in_specs=[pl.BlockSpec((None, None, tr, c), lambda k, i, core_ref: (k, core_ref[0], i, 0)), blk], out_specs=blk)
    return _pcall(body, out_shape=SDS((4, r, c), bf16), grid_spec=gs, compiler_params=_cparams(("parallel", "parallel")),
                  name=name)(core, part.reshape(4, 2, r, c), got)


def slot_sum(a, name):
    S, r, c = a.shape
    tr = _rows_tile(r, c, a.dtype.itemsize * S)

    def body(a_ref, o_ref):
        acc = a_ref[0].astype(f32)
        for s in range(1, S):
            acc = acc + a_ref[s].astype(f32)
        o_ref[...] = acc

    return _pcall(body, out_shape=SDS((r, c), f32), grid=(r // tr,), in_specs=[pl.BlockSpec((S, tr, c), lambda i: (0, i, 0))],
                  out_specs=pl.BlockSpec((tr, c), lambda i: (i, 0)), compiler_params=_cparams(("parallel",)), name=name)(a)


def reduce_scatter_start(parts, name, collective_id):
    got = exchange_sibling(parts, name + "_c")
    s1 = [pair_add(p, g, f"{name}_add{t}") for t, (p, g) in enumerate(zip(parts, got))]
    return exchange_chips_behind(s1, name + "_xy", collective_id)


def reduce_scatter_finish(refs, name):
    return [slot_sum(r[...], f"{name}_sum{t}") for t, r in enumerate(refs)]


def _to_pack(flat, dtype):
    n = flat.shape[-1]
    unit = 16 * PACK_COLS
    padded = -(-n // unit) * unit
    return jnp.pad(flat.astype(dtype), (0, padded - n)).reshape(padded // PACK_COLS, PACK_COLS)


BIG = ("w_in", "w_conv_proj", "w_gdn_proj", "w_out", "w_ffn_in", "w_ffn_out")
COL_SHARDED = ("w_in", "w_ffn_in", "conv_dw_w", "short_conv_w")
SMALL = ("b_in", "conv_dw_b", "conv_ln_g", "conv_ln_b", "b_conv_proj", "a_log", "dt_bias", "gdn_norm_w",
         "ln1_g", "ln1_b", "ln2_g", "ln2_b")
CONVW = ("conv_dw_w", "short_conv_w")
ORDER = ("w_in", "b_in", "conv_dw_w", "conv_dw_b", "conv_ln_g", "conv_ln_b", "w_conv_proj", "b_conv_proj",
         "short_conv_w", "a_log", "dt_bias", "gdn_norm_w", "w_gdn_proj", "w_out", "ln1_g", "ln1_b",
         "w_ffn_in", "w_ffn_out", "ln2_g", "ln2_b")


def _full_from_gathered(g, name):
    if name in COL_SHARDED:
        return jnp.moveaxis(g, 0, 1).reshape(g.shape[1], N_DEV * g.shape[2])
    return g.reshape(N_DEV * g.shape[1], g.shape[2])


def _w_in_perm(w, D, H):
    pad = jnp.zeros(w.shape[:-1] + (2 * LANES - 2 * H,), w.dtype)
    return jnp.concatenate([w[..., :6 * D], w[..., 6 * D + 2 * H:], w[..., 6 * D:6 * D + 2 * H], pad], axis=-1)


def _w_in_unperm(w, D, H):
    return jnp.concatenate([w[..., :6 * D], w[..., 8 * D:8 * D + 2 * H], w[..., 6 * D:8 * D]], axis=-1)


def kernel(x, w_in, b_in, conv_dw_w, conv_dw_b, conv_ln_g, conv_ln_b, w_conv_proj, b_conv_proj, short_conv_w, a_log, dt_bias, gdn_norm_w, w_gdn_proj, w_out, ln1_g, ln1_b, w_ffn_in, w_ffn_out, ln2_g, ln2_b, loss_target, m_w_in, m_b_in, m_conv_dw_w, m_conv_dw_b, m_conv_ln_g, m_conv_ln_b, m_w_conv_proj, m_b_conv_proj, m_short_conv_w, m_a_log, m_dt_bias, m_gdn_norm_w, m_w_gdn_proj, m_w_out, m_ln1_g, m_ln1_b, m_w_ffn_in, m_w_ffn_out, m_ln2_g, m_ln2_b, v_w_in, v_b_in, v_conv_dw_w, v_conv_dw_b, v_conv_ln_g, v_conv_ln_b, v_w_conv_proj, v_b_conv_proj, v_short_conv_w, v_a_log, v_dt_bias, v_gdn_norm_w, v_w_gdn_proj, v_w_out, v_ln1_g, v_ln1_b, v_w_ffn_in, v_w_ffn_out, v_ln2_g, v_ln2_b):
    W = dict(w_in=w_in, b_in=b_in, conv_dw_w=conv_dw_w, conv_dw_b=conv_dw_b, conv_ln_g=conv_ln_g, conv_ln_b=conv_ln_b,
             w_conv_proj=w_conv_proj, b_conv_proj=b_conv_proj, short_conv_w=short_conv_w, a_log=a_log, dt_bias=dt_bias,
             gdn_norm_w=gdn_norm_w, w_gdn_proj=w_gdn_proj, w_out=w_out, ln1_g=ln1_g, ln1_b=ln1_b, w_ffn_in=w_ffn_in,
             w_ffn_out=w_ffn_out, ln2_g=ln2_g, ln2_b=ln2_b)
    MO = dict(w_in=m_w_in, b_in=m_b_in, conv_dw_w=m_conv_dw_w, conv_dw_b=m_conv_dw_b, conv_ln_g=m_conv_ln_g,
              conv_ln_b=m_conv_ln_b, w_conv_proj=m_w_conv_proj, b_conv_proj=m_b_conv_proj, short_conv_w=m_short_conv_w,
              a_log=m_a_log, dt_bias=m_dt_bias, gdn_norm_w=m_gdn_norm_w, w_gdn_proj=m_w_gdn_proj, w_out=m_w_out,
              ln1_g=m_ln1_g, ln1_b=m_ln1_b, w_ffn_in=m_w_ffn_in, w_ffn_out=m_w_ffn_out, ln2_g=m_ln2_g, ln2_b=m_ln2_b)
    VO = dict(w_in=v_w_in, b_in=v_b_in, conv_dw_w=v_conv_dw_w, conv_dw_b=v_conv_dw_b, conv_ln_g=v_conv_ln_g,
              conv_ln_b=v_conv_ln_b, w_conv_proj=v_w_conv_proj, b_conv_proj=v_b_conv_proj, short_conv_w=v_short_conv_w,
              a_log=v_a_log, dt_bias=v_dt_bias, gdn_norm_w=v_gdn_norm_w, w_gdn_proj=v_w_gdn_proj, w_out=v_w_out,
              ln1_g=v_ln1_g, ln1_b=v_ln1_b, w_ffn_in=v_w_ffn_in, w_ffn_out=v_w_ffn_out, ln2_g=v_ln2_g, ln2_b=v_ln2_b)

    _, T, D = x.shape
    DEPTH = w_in.shape[0]
    H = D // HEAD_DIM
    F = w_ffn_out.shape[1] * N_DEV
    alpha = (2.0 * DEPTH) ** 0.25
    lnres_fn = make_lnres_fn(alpha)
    NA = 8 * D + 2 * LANES
    TT = min(256, T)
    dev = 4 * lax.axis_index("x") + 2 * lax.axis_index("y") + lax.axis_index("c")

    cw_sizes = [(n, W[n].shape[1:]) for n in CONVW]
    cw_flat = jnp.concatenate([W[n].reshape(-1) for n in CONVW])
    cw_g = all_gather([_to_pack(cw_flat, f32)], "ag_convw")[0].reshape(N_DEV, -1)
    conv_full, off = {}, 0
    for n in CONVW:
        sz = W[n].size
        blk = cw_g[:, off:off + sz].reshape((N_DEV,) + W[n].shape)
        conv_full[n] = jnp.moveaxis(blk, 0, 2).reshape(W[n].shape[0], W[n].shape[1], -1)
        off += sz

    def gather_layer_start(l):
        return all_gather_behind([W[n][l].astype(bf16) for n in BIG], f"ag_w{l}", l)

    def gather_layer_finish(refs, after):
        vals, _ = lax.optimization_barrier(([r[...] for r in refs], after))
        g = dict(zip(BIG, vals))
        out = {n: g[n].reshape(-1, g[n].shape[2]) for n in BIG if n not in COL_SHARDED}
        out["w_ffn_in"] = g["w_ffn_in"]
        out["w_in"] = _w_in_perm(_full_from_gathered(g["w_in"], "w_in"), D, H)
        return out

    def lane_vec(v, off=0):
        return jnp.pad(v, (off, LANES - off - v.shape[0])).reshape(1, LANES)

    def row(v):
        return v.reshape(1, -1)

    h32 = x.reshape(T, D)
    h16 = h32.astype(bf16)
    saved = []
    gathering = gather_layer_start(0)
    for l in range(DEPTH):
        G = gather_layer_finish(gathering, h32)
        if l + 1 < DEPTH:
            gathering = gather_layer_start(l + 1)
        b_all = _w_in_perm(b_in[l], D, H)
        alog_vec, dtb_vec = lane_vec(a_log[l], H), lane_vec(dt_bias[l], H)
        nw = row(gdn_norm_w[l])
        z = matmul(h16, G["w_in"], "nn", f"l{l}_mm_in", bias=b_all)
        c0, = rowwise(glu_fn, [(z, 0, D), (z, 1, D)], [], [(f32,)], TT, f"l{l}_glu")
        c1 = conv_fwd(c0, 0, conv_full["conv_dw_w"][l], 0, D, CONV_WIDTH, row(conv_dw_b[l]), f"l{l}_conv")
        c3, = rowwise(lnsilu_fn, [(c1, 0, D)], [row(conv_ln_g[l]), row(conv_ln_b[l])], [(bf16,)], TT, f"l{l}_lnsilu")
        yc = matmul(c3, G["w_conv_proj"], "nn", f"l{l}_mm_cp", bias=b_conv_proj[l])
        qkv = conv_fwd(z, 2 * D, conv_full["short_conv_w"][l], 0, 3 * D, SHORT_CONV, None, f"l{l}_sconv", tt=1024)
        og, ssave, isave = gdn_fwd(qkv, z, 5 * D, 8 * D, alog_vec, dtb_vec, nw, f"l{l}_gdn")
        yg = matmul(og, G["w_gdn_proj"], "nn", f"l{l}_mm_gp")
        m, = rowwise(merge_fn, [(z, 6, D), (z, 7, D), (yc, 0, D), (yg, 0, D)], [], [(bf16,)], TT, f"l{l}_merge")
        mix = matmul(m, G["w_out"], "nn", f"l{l}_mm_out")
        x1_32, x1_16 = rowwise(lnres_fn, [(h32, 0, D), (mix, 0, D)], [row(ln1_g[l]), row(ln1_b[l])], [(f32, bf16)], TT, f"l{l}_ln1")
        hf = matmul(x1_16, G["w_ffn_in"], "nn", f"l{l}_mm_fi", b_blocked=True)
        act, = rowwise(swiglu_fn, [(hf, 0, F), (hf, 1, F)], [], [(bf16,)], min(128, T), f"l{l}_swiglu")
        ff = matmul(act, G["w_ffn_out"], "nn", f"l{l}_mm_fo")
        x2_32, x2_16 = rowwise(lnres_fn, [(x1_32, 0, D), (ff, 0, D)], [row(ln2_g[l]), row(ln2_b[l])], [(f32, bf16)], TT, f"l{l}_ln2")
        saved.append(dict(G=G, h32=h32, h16=h16, z=z, c0=c0, c1=c1, c3=c3, yc=yc, qkv=qkv, ssave=ssave, isave=isave, og=og, yg=yg, m=m,
                          mix=mix, x1_32=x1_32, x1_16=x1_16, hf=hf, act=act, ff=ff, alog_vec=alog_vec, dtb_vec=dtb_vec, nw=nw))
        h32, h16 = x2_32, x2_16

    loss_vec, dy = loss_and_grad(h32, loss_target.reshape(T, D), "loss")
    loss = lax.psum(loss_vec[0, 0], AXES)

    gsmall = {n: [None] * DEPTH for n in SMALL + CONVW}
    gbig = {n: [None] * DEPTH for n in BIG}
    dh = [dy]
    reducing = []

    def finish_reduce(l, refs):
        for n, red in zip(BIG, reduce_scatter_finish(refs, f"rs{l}")):
            gbig[n][l] = red

    for l in reversed(range(DEPTH)):
        s = saved[l]
        G = s["G"]
        TB = min(128, T)
        d_x1r, d_ff, dg2, db2 = rowwise_vjp(lnres_fn, [(s["x1_32"], 0, D), (s["ff"], 0, D)], [row(ln2_g[l]), row(ln2_b[l])],
                                            [dh], [f32, bf16], TB, f"l{l}_ln2_b")
        d_act = matmul(d_ff, G["w_ffn_out"], "nt", f"l{l}_mm_fo_dx", tn_cap=1408)
        dw_fo = matmul(s["act"], d_ff, "tn", f"l{l}_mm_fo_dw", tm_cap=1408, tn_cap=2048, tk_cap=1024, out_dtype=bf16)
        d_hf, = rowwise_vjp(swiglu_fn, [(s["hf"], 0, F), (s["hf"], 1, F)], [], [[d_act]], [bf16, bf16], min(64, T), f"l{l}_swiglu_b",
                            pack=True)
        d_x1m = matmul(d_hf, G["w_ffn_in"], "nt", f"l{l}_mm_fi_dx", b_blocked=True)
        dw_fi = matmul(s["x1_16"], d_hf, "tn", f"l{l}_mm_fi_dw", tk_cap=1024, out_dtype=bf16, out_blocked=True)
        d_hr, d_mix, dg1, db1 = rowwise_vjp(lnres_fn, [(s["h32"], 0, D), (s["mix"], 0, D)], [row(ln1_g[l]), row(ln1_b[l])],
                                            [[d_x1r, d_x1m]], [f32, bf16], TB, f"l{l}_ln1_b")
        d_m = matmul(d_mix, G["w_out"], "nt", f"l{l}_mm_out_dx")
        dw_out = matmul(s["m"], d_mix, "tn", f"l{l}_mm_out_dw", tk_cap=1024, out_dtype=bf16)
        d_ga, d_gb, d_yc, d_yg = rowwise_vjp(merge_fn, [(s["z"], 6, D), (s["z"], 7, D), (s["yc"], 0, D), (s["yg"], 0, D)], [],
                                             [[d_m]], [bf16] * 4, TB, f"l{l}_merge_b")
        d_c3 = matmul(d_yc, G["w_conv_proj"], "nt", f"l{l}_mm_cp_dx")
        dw_cp = matmul(s["c3"], d_yc, "tn", f"l{l}_mm_cp_dw", tk_cap=1024, out_dtype=bf16)
        db_cp = colsum(d_yc, f"l{l}_cs_cp")
        d_c1, dcg, dcb = rowwise_vjp(lnsilu_fn, [(s["c1"], 0, D)], [row(conv_ln_g[l]), row(conv_ln_b[l])], [[d_c3]], [f32], TB, f"l{l}_lnsilu_b")
        d_c0, dw31, db31 = conv_bwd(d_c1, s["c0"], 0, conv_full["conv_dw_w"][l], 0, D, CONV_WIDTH, f32, f"l{l}_conv_b")
        d_glu_a, d_glu_b = rowwise_vjp(glu_fn, [(s["z"], 0, D), (s["z"], 1, D)], [], [[d_c0]], [bf16, bf16], TB, f"l{l}_glu_b")
        d_og = matmul(d_yg, G["w_gdn_proj"], "nt", f"l{l}_mm_gp_dx")
        dw_gp = matmul(s["og"], d_yg, "tn", f"l{l}_mm_gp_dw", tk_cap=1024, out_dtype=bf16)
        dq, dk, dv, d_zg, d_zs, dalog, ddtb, dnw = gdn_bwd(d_og, s["qkv"], s["z"], 5 * D, 8 * D, s["alog_vec"], s["dtb_vec"], s["nw"],
                                                            s["ssave"], s["isave"], f"l{l}_gdn_b")
        dxs, dwss = [], []
        for sec, dsec in enumerate((dq, dk, dv)):
            dxp, dwp, _ = conv_bwd(dsec, s["z"], (2 + sec) * D, conv_full["short_conv_w"][l], sec * D, D, SHORT_CONV, bf16,
                                   f"l{l}_sconv_b{sec}", tt=1024)
            dxs.append(dxp)
            dwss.append(dwp)
        dz = jnp.concatenate([d_glu_a, d_glu_b] + dxs + [d_zg, d_ga, d_gb, d_zs.astype(bf16), jnp.zeros((T, LANES), bf16)], axis=1)
        d_hm = matmul(dz, G["w_in"], "nt", f"l{l}_mm_in_dx", tk_cap=3328)
        dw_in = matmul(s["h16"], dz, "tn", f"l{l}_mm_in_dw", tk_cap=1024, out_dtype=bf16)
        db_all = colsum(dz, f"l{l}_cs_in")
        dh = [d_hr, d_hm]

        gsmall["b_in"][l] = _w_in_unperm(db_all[0], D, H)
        gsmall["conv_dw_b"][l] = db31[0]
        gsmall["conv_ln_g"][l], gsmall["conv_ln_b"][l] = dcg[0], dcb[0]
        gsmall["b_conv_proj"][l] = db_cp[0]
        gsmall["a_log"][l], gsmall["dt_bias"][l] = dalog[0, H:2 * H], ddtb[0, H:2 * H]
        gsmall["gdn_norm_w"][l] = dnw[0]
        gsmall["ln1_g"][l], gsmall["ln1_b"][l], gsmall["ln2_g"][l], gsmall["ln2_b"][l] = dg1[0], db1[0], dg2[0], db2[0]
        gsmall["conv_dw_w"][l] = dw31
        gsmall["short_conv_w"][l] = jnp.concatenate(dwss, axis=1)

        dw_in_u = _w_in_unperm(dw_in, D, H)
        blocks = dict(w_in=jnp.moveaxis(dw_in_u.reshape(D, N_DEV, -1), 1, 0), w_ffn_in=dw_fi)
        for n, dw in (("w_conv_proj", dw_cp), ("w_gdn_proj", dw_gp), ("w_out", dw_out), ("w_ffn_out", dw_fo)):
            blocks[n] = dw.reshape(N_DEV, -1, dw.shape[1])
        reducing.append((l, reduce_scatter_start([blocks[n] for n in BIG], f"rs{l}", DEPTH + l)))
        if len(reducing) > 1:
            finish_reduce(*reducing.pop(0))
    finish_reduce(*reducing.pop(0))

    grad_x, = rowwise(add_fn, [(dh[0], 0, D), (dh[1], 0, D)], [], [(f32,)], TT, "grad_x")
    grad_x = grad_x.reshape(1, T, D)

    small_flat = jnp.concatenate([jnp.stack(gsmall[n]).reshape(-1) for n in SMALL + CONVW])
    sg = all_gather([_to_pack(small_flat, f32)], "ag_small")[0]
    small_tot = slot_sum(sg, "small_sum").reshape(-1)
    grads, off = {}, 0
    for n in SMALL:
        grads[n] = small_tot[off:off + W[n].size].reshape(W[n].shape)
        off += W[n].size
    for n in CONVW:
        L_, K_, c_ = W[n].shape
        fullg = small_tot[off:off + L_ * K_ * c_ * N_DEV].reshape(L_, K_, c_ * N_DEV)
        grads[n] = lax.dynamic_slice_in_dim(fullg, dev * c_, c_, axis=2)
        off += L_ * K_ * c_ * N_DEV
    for n in BIG:
        grads[n] = jnp.stack(gbig[n])

    delta, new_m, new_v = {}, {}, {}
    for n in BIG + CONVW:
        delta[n], new_m[n], new_v[n] = adamw(W[n], grads[n], MO[n], VO[n], f"adamw_{n}")
    pk = lambda d: _to_pack(jnp.concatenate([d[n].reshape(-1) for n in SMALL]), f32)[None]
    ds, ms, vs = adamw(pk(W), pk(grads), pk(MO), pk(VO), "adamw_small")
    off = 0
    for n in SMALL:
        sl = lambda a: a.reshape(-1)[off:off + W[n].size].reshape(W[n].shape)
        delta[n], new_m[n], new_v[n] = sl(ds), sl(ms), sl(vs)
        off += W[n].size

    return (loss, grad_x, *[grads[n] for n in ORDER], *[delta[n] for n in ORDER],
            *[new_m[n] for n in ORDER], *[new_v[n] for n in ORDER])
```

```python
import jax
import jax.numpy as jnp
from jax import lax
from jax.experimental import pallas as pl
from jax.experimental.pallas import tpu as pltpu
from jax.experimental.pallas import tpu_sc as plsc

f32, bf16 = jnp.float32, jnp.bfloat16
SDS = jax.ShapeDtypeStruct
MESH = pl.DeviceIdType.MESH
AXES = ("x", "y", "c")
N_DEV = 8

CONV_WIDTH = 31
SHORT_CONV = 4
HEAD_DIM = 128
CHUNK = 64
LN_EPS = 1e-5
ADAM_LR, ADAM_B1, ADAM_B2, ADAM_EPS, ADAM_WD, ADAM_STEP = 0.001, 0.9, 0.999, 1e-08, 0.01, 10

LANES = 128
HALO = 32
PACK_COLS = 1024
VMEM_LIMIT = 56 * 1024 * 1024


def _pcall(body, **kw):
    return pl.pallas_call(body, **kw)


def _cparams(sem=None):
    return pltpu.CompilerParams(dimension_semantics=sem, vmem_limit_bytes=VMEM_LIMIT)


def _pick(n, cap, mult):
    if n <= cap:
        return n
    for t in range(cap - cap % mult, 0, -mult):
        if n % t == 0:
            return t
    raise ValueError(f"no tile for {n} under {cap} in steps of {mult}")


MXU_DIM = 256


def _pick_mxu(n, cap, mult):
    if n % MXU_DIM:
        return _pick(n, cap, mult)
    wide, fine = _pick(n, cap, MXU_DIM), _pick(n, cap, max(mult, LANES))
    return fine if 2 * fine >= 3 * wide else wide


def matmul(a, b, mode, name, bias=None, out_dtype=f32, tm_cap=1024, tn_cap=1280, tk_cap=2048,
           b_blocked=False, out_blocked=False, colsum_a=False):
    if b_blocked:
        nb, br, bc = b.shape
        b2 = (br, nb * bc)
    else:
        b2 = b.shape
    if mode == "nn":
        (M, K), (K2, N) = a.shape, b2
    elif mode == "nt":
        (M, K), (N, K2) = a.shape, b2
    else:
        (K, M), (K2, N) = a.shape, b2
    assert K == K2, (a.shape, b.shape, mode)
    tm = _pick_mxu(M, tm_cap, 8)
    tn = _pick_mxu(N, tn_cap, LANES)
    tk = _pick_mxu(K, tk_cap, LANES)
    if b_blocked and mode == "nn":
        tn = N // N_DEV
    if b_blocked and mode == "nt":
        tk = K // N_DEV
    if out_blocked:
        tn = N // N_DEV
    nm, nn, nk = M // tm, N // tn, K // tk
    dims = {"nn": (((1,), (0,)), ((), ())), "nt": (((1,), (1,)), ((), ())), "tn": (((0,), (0,)), ((), ()))}[mode]
    has_bias = bias is not None

    def body(*refs):
        a_ref, b_ref = refs[0], refs[1]
        bias_ref = refs[2] if has_bias else None
        o_ref = refs[2 + has_bias]
        prod = lax.dot_general(a_ref[...], b_ref[...], dims, preferred_element_type=f32)
        if colsum_a:
            cs_ref = refs[3 + has_bias]
            gi_, gj_, gk_ = gi(pl.program_id(0), pl.program_id(1), pl.program_id(2))

            @pl.when(jnp.logical_and(gi_ == 0, jnp.logical_and(gj_ == 0, gk_ == 0)))
            def _():
                cs_ref[...] = jnp.zeros_like(cs_ref)

            @pl.when(gj_ == 0)
            def _():
                cols = pl.ds(pl.multiple_of(gk_ * tk, LANES), tk)
                cs_ref[:, cols] += jnp.sum(a_ref[...].astype(f32), axis=0, keepdims=True)

        def finish(acc):
            if has_bias:
                acc = acc + bias_ref[...]
            o_ref[...] = acc.astype(out_dtype)

        if nk == 1:
            finish(prod)
        else:
            acc_ref = refs[3 + has_bias + colsum_a]
            k = pl.program_id(2)

            @pl.when(k == 0)
            def _():
                acc_ref[...] = prod

            @pl.when(jnp.logical_and(k > 0, k < nk - 1))
            def _():
                acc_ref[...] += prod

            @pl.when(k == nk - 1)
            def _():
                finish(acc_ref[...] + prod)

    a_bytes, b_bytes = M * K, N * K
    m_outer = a_bytes >= b_bytes
    if m_outer:
        grid = (nm, nn, nk)
        gi = lambda i, j, k: (i, j, k)
    else:
        grid = (nn, nm, nk)
        gi = lambda j, i, k: (i, j, k)

    def amap(*g):
        i, j, k = gi(*g)
        return (k, i) if mode == "tn" else (i, k)

    def bmap(*g):
        i, j, k = gi(*g)
        return (j, k) if mode == "nt" else (k, j)

    def omap(*g):
        i, j, k = gi(*g)
        return (i, j)

    def biasmap(*g):
        i, j, k = gi(*g)
        return (0, j)

    def bmap_blocked(*g):
        i, j, k = gi(*g)
        return (k, j, 0) if mode == "nt" else (j, k, 0)

    def omap_blocked(*g):
        i, j, k = gi(*g)
        return (j, i, 0)

    b_block = (tn, tk) if mode == "nt" else (tk, tn)
    in_specs = [pl.BlockSpec((tk, tm) if mode == "tn" else (tm, tk), amap),
                pl.BlockSpec((None,) + b_block, bmap_blocked) if b_blocked else pl.BlockSpec(b_block, bmap)]
    args = [a, b]
    if has_bias:
        in_specs.append(pl.BlockSpec((1, tn), biasmap))
        args.append(bias.reshape(1, N).astype(f32))
    if out_blocked:
        out_shape, out_spec = SDS((N_DEV, M, tn), out_dtype), pl.BlockSpec((None, tm, tn), omap_blocked)
    else:
        out_shape, out_spec = SDS((M, N), out_dtype), pl.BlockSpec((tm, tn), omap)
    sem = ("parallel", "parallel", "arbitrary")
    if colsum_a:
        assert mode == "nt"
        out_shape, out_spec = [out_shape, SDS((1, K), f32)], [out_spec, pl.BlockSpec((1, K), lambda *g: (0, 0))]
        sem = ("arbitrary", "arbitrary", "arbitrary")
    return _pcall(body, out_shape=out_shape, grid=grid, in_specs=in_specs,
                  out_specs=out_spec,
                  scratch_shapes=[pltpu.VMEM((tm, tn), f32)] if nk > 1 else [],
                  compiler_params=_cparams(sem), name=name)(*args)


def _row_specs(rows, tt):
    specs, args = [], []
    for arr, cb, width in rows:
        specs.append(pl.BlockSpec((tt, width), lambda i, cb=cb: (i, cb)))
        args.append(arr)
    return specs, args


def _param_specs(params):
    return [pl.BlockSpec(p.shape, lambda i: (0, 0)) for p in params]


def rowwise(fn, rows, params, out_dtypes, tt, name):
    T = rows[0][0].shape[0]
    nr, npar = len(rows), len(params)
    blocks = [SDS((tt, w), f32) for _, _, w in rows] + [SDS(p.shape, f32) for p in params]
    outs = jax.eval_shape(fn, *blocks)
    out_shape, out_specs = [], []
    for o, dts in zip(outs, out_dtypes):
        for dt in dts:
            out_shape.append(SDS((T, o.shape[1]), dt))
            out_specs.append(pl.BlockSpec((tt, o.shape[1]), lambda i: (i, 0)))

    def body(*refs):
        xs = [r[...].astype(f32) for r in refs[:nr + npar]]
        res = fn(*xs)
        k = nr + npar
        for o, dts in zip(res, out_dtypes):
            for dt in dts:
                refs[k][...] = o.astype(dt)
                k += 1

    rspecs, rargs = _row_specs(rows, tt)
    return _pcall(body, out_shape=out_shape, grid=(T // tt,), in_specs=rspecs + _param_specs(params),
                  out_specs=out_specs, compiler_params=_cparams(("parallel",)), name=name)(*rargs, *params)


def rowwise_vjp(fn, rows, params, cots, d_dtypes, tt, name, pack=False):
    T = rows[0][0].shape[0]
    nr, npar = len(rows), len(params)
    ncot = [len(c) for c in cots]
    flat_cots = [c for cs in cots for c in cs]

    def body(*refs):
        i = pl.program_id(0)
        xs = [r[...].astype(f32) for r in refs[:nr + npar]]
        k = nr + npar
        cs = []
        for n in ncot:
            tot = refs[k][...].astype(f32)
            for r in refs[k + 1:k + n]:
                tot = tot + r[...].astype(f32)
            cs.append(tot)
            k += n
        _, pull = jax.vjp(fn, *xs)
        grads = pull(tuple(cs))
        if pack:
            refs[k][...] = jnp.concatenate([g.astype(d_dtypes[0]) for g in grads[:nr]], axis=1)
            k += 1
        else:
            for g, dt in zip(grads[:nr], d_dtypes):
                refs[k][...] = g.astype(dt)
                k += 1
        for g in grads[nr:]:
            ref = refs[k]
            k += 1

            @pl.when(i == 0)
            def _(ref=ref, g=g):
                ref[...] = g

            @pl.when(i > 0)
            def _(ref=ref, g=g):
                ref[...] += g

    rspecs, rargs = _row_specs(rows, tt)
    cot_specs = [pl.BlockSpec((tt, c.shape[1]), lambda i: (i, 0)) for c in flat_cots]
    widths = [sum(w for _, _, w in rows)] if pack else [w for _, _, w in rows]
    out_shape = [SDS((T, w), dt) for w, dt in zip(widths, d_dtypes)] + [SDS(p.shape, f32) for p in params]
    out_specs = [pl.BlockSpec((tt, w), lambda i: (i, 0)) for w in widths] + _param_specs(params)
    return _pcall(body, out_shape=out_shape, grid=(T // tt,), in_specs=rspecs + _param_specs(params) + cot_specs,
                  out_specs=out_specs, compiler_params=_cparams(("arbitrary",)), name=name)(*rargs, *params, *flat_cots)


def _sigmoid(x):
    return jax.nn.sigmoid(x)


def _silu(x):
    return x * jax.nn.sigmoid(x)


def _softplus(x):
    return jnp.maximum(x, 0.0) + jnp.log(1.0 + jnp.exp(-jnp.abs(x)))


def _layer_norm(x, g, b):
    mu = jnp.mean(x, axis=-1, keepdims=True)
    xc = x - mu
    var = jnp.mean(xc * xc, axis=-1, keepdims=True)
    return xc * lax.rsqrt(var + LN_EPS) * g + b


def glu_fn(a, b):
    return (a * _sigmoid(b),)


def lnsilu_fn(c, g, b):
    return (_silu(_layer_norm(c, g, b)),)


def merge_fn(ga, gb, yc, yg):
    return (_sigmoid(ga) * yc + _sigmoid(gb) * yg,)


def swiglu_fn(gate, up):
    return (_silu(gate) * up,)


def make_lnres_fn(alpha):
    def lnres_fn(h, y, g, b):
        return (_layer_norm(alpha * h + y, g, b),)
    return lnres_fn


def add_fn(a, b):
    return (a + b,)


CONV_ROWS = 64


def _tap_groups(K, first_row):
    groups = {}
    for j in range(K):
        groups.setdefault((first_row + j) % 8, []).append(j)
    out = []
    for taps in groups.values():
        start = first_row + taps[0]
        out.append((start, 8 * (len(taps) - 1), [(j, first_row + j - start) for j in taps]))
    return out


def conv_fwd(x, x_col0, w, w_col0, width, K, bias, name, tt=512, tc=256):
    T = x.shape[0]
    tt = min(tt, T)
    assert width % tc == 0 and x_col0 % tc == 0 and w_col0 % tc == 0 and tt % HALO == 0 and tt % CONV_ROWS == 0
    hb = tt // HALO
    xb, wb = x_col0 // tc, w_col0 // tc
    has_bias = bias is not None
    groups = _tap_groups(K, HALO - (K - 1))

    def body(*refs):
        x_ref, halo_ref, w_ref = refs[:3]
        b_ref = refs[3] if has_bias else None
        o_ref, ext_ref = refs[3 + has_bias], refs[4 + has_bias]
        i = pl.program_id(1)
        ext_ref[pl.ds(0, HALO), :] = jnp.where(i > 0, halo_ref[...], 0.0)
        ext_ref[pl.ds(HALO, tt), :] = x_ref[...]
        for r in range(tt // CONV_ROWS):
            acc = jnp.zeros((CONV_ROWS, tc), f32)
            for start, extra, taps in groups:
                win = ext_ref[pl.ds(r * CONV_ROWS + start, CONV_ROWS + extra), :]
                for j, off in taps:
                    acc = acc + w_ref[j:j + 1, :] * win[off:off + CONV_ROWS]
            if has_bias:
                acc = acc + b_ref[...]
            o_ref[pl.ds(r * CONV_ROWS, CONV_ROWS), :] = acc

    in_specs = [pl.BlockSpec((tt, tc), lambda cb, i: (i, xb + cb)),
                pl.BlockSpec((HALO, tc), lambda cb, i: (jnp.maximum(i * hb - 1, 0), xb + cb)),
                pl.BlockSpec((K, tc), lambda cb, i: (0, wb + cb))]
    args = [x, x, w]
    if has_bias:
        in_specs.append(pl.BlockSpec((1, tc), lambda cb, i: (0, cb)))
        args.append(bias)
    return _pcall(body, out_shape=SDS((T, width), f32), grid=(width // tc, T // tt), in_specs=in_specs,
                  out_specs=pl.BlockSpec((tt, tc), lambda cb, i: (i, cb)),
                  scratch_shapes=[pltpu.VMEM((tt + HALO, tc), f32)],
                  compiler_params=_cparams(("parallel", "arbitrary")), name=name)(*args)


def conv_bwd(dy, x, x_col0, w, w_col0, width, K, dx_dtype, name, tt=512, tc=256):
    T = x.shape[0]
    tt = min(tt, T)
    hb = tt // HALO
    nt = T // tt
    xb, wb = x_col0 // tc, w_col0 // tc
    x_groups = _tap_groups(K, HALO - (K - 1))
    dy_groups = [(start, extra, [(K - 1 - o, off) for o, off in taps]) for start, extra, taps in _tap_groups(K, 0)]
    RC = CONV_ROWS

    def body(dy_ref, dyn_ref, x_ref, halo_ref, w_ref, dx_ref, dw_ref, db_ref, xext_ref, dyext_ref, dwacc_ref, dbacc_ref):
        i = pl.program_id(1)

        @pl.when(i == 0)
        def _():
            dwacc_ref[...] = jnp.zeros_like(dwacc_ref)
            dbacc_ref[...] = jnp.zeros_like(dbacc_ref)

        xext_ref[pl.ds(0, HALO), :] = jnp.where(i > 0, halo_ref[...], 0.0)
        xext_ref[pl.ds(HALO, tt), :] = x_ref[...]
        dyext_ref[pl.ds(0, tt), :] = dy_ref[...].astype(f32)
        dyext_ref[pl.ds(tt, HALO), :] = jnp.where(i < nt - 1, dyn_ref[...].astype(f32), 0.0)

        def fold(p):
            return jnp.sum(p.reshape(RC // 8, 8, tc), axis=0)

        for r in range(tt // RC):
            acc = jnp.zeros((RC, tc), f32)
            for start, extra, taps in dy_groups:
                win = dyext_ref[pl.ds(r * RC + start, RC + extra), :]
                for j, off in taps:
                    acc = acc + w_ref[j:j + 1, :] * win[off:off + RC]
            dx_ref[pl.ds(r * RC, RC), :] = acc.astype(dx_dtype)
            dyc = dyext_ref[pl.ds(r * RC, RC), :]
            dbacc_ref[...] += fold(dyc)
            for start, extra, taps in x_groups:
                win = xext_ref[pl.ds(r * RC + start, RC + extra), :]
                for j, off in taps:
                    dwacc_ref[j] += fold(dyc * win[off:off + RC])

        @pl.when(i == nt - 1)
        def _():
            dw_ref[...] = jnp.sum(dwacc_ref[...], axis=1)
            db_ref[...] = jnp.sum(dbacc_ref[...], axis=0, keepdims=True)

    in_specs = [pl.BlockSpec((tt, tc), lambda cb, i: (i, cb)),
                pl.BlockSpec((HALO, tc), lambda cb, i: (jnp.minimum((i + 1) * hb, nt * hb - 1), cb)),
                pl.BlockSpec((tt, tc), lambda cb, i: (i, xb + cb)),
                pl.BlockSpec((HALO, tc), lambda cb, i: (jnp.maximum(i * hb - 1, 0), xb + cb)),
                pl.BlockSpec((K, tc), lambda cb, i: (0, wb + cb))]
    out_shape = [SDS((T, width), dx_dtype), SDS((K, width), f32), SDS((1, width), f32)]
    out_specs = [pl.BlockSpec((tt, tc), lambda cb, i: (i, cb)), pl.BlockSpec((K, tc), lambda cb, i: (0, cb)),
                 pl.BlockSpec((1, tc), lambda cb, i: (0, cb))]
    return _pcall(body, out_shape=out_shape, grid=(width // tc, nt), in_specs=in_specs, out_specs=out_specs,
                  scratch_shapes=[pltpu.VMEM((tt + HALO, tc), f32), pltpu.VMEM((tt + HALO, tc), f32),
                                  pltpu.VMEM((K, 8, tc), f32), pltpu.VMEM((8, tc), f32)],
                  compiler_params=_cparams(("parallel", "arbitrary")), name=name)(dy, dy, x, x, w)


def _mm_b(eq, a, b):
    return jnp.einsum(eq, a.astype(bf16), b.astype(bf16), preferred_element_type=f32)


def _split_bf16(a):
    hi = a.astype(bf16)
    return hi, (a - hi.astype(f32)).astype(bf16)


def _mm_3(eq, a, b):
    ah, al = _split_bf16(a)
    bh, bl = _split_bf16(b)
    e = lambda x, y: jnp.einsum(eq, x, y, preferred_element_type=f32)
    return e(ah, bh) + (e(ah, bl) + e(al, bh))


def _unit_lower_inverse(L):
    C = L.shape[-1]
    ii = lax.broadcasted_iota(jnp.int32, (C, C), 0)
    jj = lax.broadcasted_iota(jnp.int32, (C, C), 1)
    P = -L
    inv = (ii == jj).astype(f32)[None] + P
    span = 2
    while span < C:
        P = _mm_3("hij,hjk->hik", P, P)
        inv = inv + _mm_3("hij,hjk->hik", inv, P)
        span *= 2
    return inv


@jax.custom_vjp
def _known_inverse(L, inv):
    return inv


def _known_inverse_fwd(L, inv):
    return inv, inv


def _known_inverse_bwd(inv, g):
    t = _mm_3("hji,hjk->hik", inv, g)
    return -_mm_3("hij,hkj->hik", t, inv), jnp.zeros_like(inv)


_known_inverse.defvjp(_known_inverse_fwd, _known_inverse_bwd)


def _gdn_chunk(n_heads, head0, inv_known, S, qr, kr, vr, zg, zs, alog, dtb, nw):
    HB, C, dk = qr.shape
    q = _silu(qr)
    k = _silu(kr)
    v = _silu(vr)
    q = q * lax.rsqrt(jnp.sum(q * q, axis=-1, keepdims=True) + 1e-6) * (dk ** -0.5)
    k = k * lax.rsqrt(jnp.sum(k * k, axis=-1, keepdims=True) + 1e-6)
    beta_all = _sigmoid(zs)
    la_all = -jnp.exp(alog) * _softplus(zs + dtb)
    lane = lax.broadcasted_iota(jnp.int32, (C, LANES), 1)
    betas, las = [], []
    for h in range(HB):
        betas.append(jnp.sum(jnp.where(lane == head0 + h, beta_all, 0.0), axis=-1, keepdims=True))
        las.append(jnp.sum(jnp.where(lane == n_heads + head0 + h, la_all, 0.0), axis=-1, keepdims=True))
    beta = jnp.stack(betas, axis=0)
    la = jnp.stack(las, axis=0)
    ii = lax.broadcasted_iota(jnp.int32, (C, C), 0)
    jj = lax.broadcasted_iota(jnp.int32, (C, C), 1)
    eye = (ii == jj).astype(f32)[None]
    causal = (jj <= ii)[None]
    strict = (jj < ii)[None]
    la_row = jnp.sum(la * eye, axis=1, keepdims=True)
    g_col = jnp.sum(jnp.where(causal, la_row, 0.0), axis=2, keepdims=True)
    g_row = jnp.sum(jnp.where((ii <= jj)[None], la, 0.0), axis=1, keepdims=True)
    g_last = jnp.sum(la, axis=1, keepdims=True)
    decay = jnp.where(causal, jnp.exp(jnp.where(causal, g_col - g_row, 0.0)), 0.0)
    kb = k * beta
    L = jnp.where(strict, _mm_b("hid,hjd->hij", kb, k) * decay, 0.0)
    inv = _unit_lower_inverse(L) if inv_known is None else _known_inverse(L, inv_known)
    eg = jnp.exp(g_col)
    u = _mm_b("hij,hjd->hid", inv, v * beta)
    w = _mm_b("hij,hjd->hid", inv, kb * eg)
    attn = jnp.where(causal, _mm_b("hid,hjd->hij", q, k) * decay, 0.0)
    qd = q * eg
    kd = k * jnp.exp(g_last - g_col)
    v_new = u - _mm_b("hck,hkv->hcv", w, S)
    o = _mm_b("hck,hkv->hcv", qd, S) + _mm_b("hcj,hjv->hcv", attn, v_new)
    S_new = S * jnp.exp(g_last) + _mm_b("hck,hcv->hkv", kd, v_new)
    o = o * lax.rsqrt(jnp.mean(o * o, axis=-1, keepdims=True) + 1e-6) * nw * _silu(zg)
    return S_new, o, inv


def _split_heads(x, HB):
    return jnp.stack([x[:, h * HEAD_DIM:(h + 1) * HEAD_DIM] for h in range(HB)], axis=0)


def _merge_heads(x):
    return jnp.concatenate([x[h] for h in range(x.shape[0])], axis=-1)


def _gdn_dims(T, D):
    H = D // HEAD_DIM
    HB = min(8, H)
    tt = min(256, T)
    return H, HB, tt, tt // CHUNK


def gdn_fwd(qkv, z, zg_col0, zs_col0, alog_vec, dtb_vec, nw, name):
    T, D3 = qkv.shape
    D = D3 // 3
    H, HB, tt, nc = _gdn_dims(T, D)
    W = HB * HEAD_DIM
    nhg = H // HB
    zgb, zsb = zg_col0 // W, zs_col0 // LANES

    def body(q_ref, k_ref, v_ref, zg_ref, zs_ref, alog_ref, dtb_ref, nw_ref, o_ref, ssave_ref, isave_ref, s_ref):
        i, hg = pl.program_id(0), pl.program_id(1)
        hsl = pl.ds(hg * HB, HB)

        @pl.when(i == 0)
        def _():
            s_ref[hsl] = jnp.zeros((HB, HEAD_DIM, HEAD_DIM), f32)

        alog, dtb, nwv = alog_ref[...], dtb_ref[...], nw_ref[...]

        def step(c, carry):
            rows = pl.ds(pl.multiple_of(c * CHUNK, CHUNK), CHUNK)
            S = s_ref[hsl]
            ssave_ref[:, pl.ds(c, 1)] = S[:, None]
            S_new, o, inv = _gdn_chunk(H, hg * HB, None, S, _split_heads(q_ref[rows, :], HB),
                                       _split_heads(k_ref[rows, :], HB), _split_heads(v_ref[rows, :], HB),
                                       _split_heads(zg_ref[rows, :], HB), zs_ref[rows, :], alog, dtb, nwv)
            s_ref[hsl] = S_new
            isave_ref[:, pl.ds(c, 1)] = inv[:, None]
            o_ref[rows, :] = _merge_heads(o).astype(bf16)
            return carry

        lax.fori_loop(0, nc, step, 0)

    col = lambda off: pl.BlockSpec((tt, W), lambda i, hg, off=off: (i, off + hg))
    in_specs = [col(0), col(nhg), col(2 * nhg), col(zgb), pl.BlockSpec((tt, LANES), lambda i, hg: (i, zsb)),
                pl.BlockSpec((1, LANES), lambda i, hg: (0, 0)), pl.BlockSpec((1, LANES), lambda i, hg: (0, 0)),
                pl.BlockSpec((1, LANES), lambda i, hg: (0, 0))]
    out_shape = [SDS((T, D), bf16), SDS((H, T // CHUNK, HEAD_DIM, HEAD_DIM), f32), SDS((H, T // CHUNK, CHUNK, CHUNK), f32)]
    out_specs = [pl.BlockSpec((tt, W), lambda i, hg: (i, hg)),
                 pl.BlockSpec((HB, nc, HEAD_DIM, HEAD_DIM), lambda i, hg: (hg, i, 0, 0)),
                 pl.BlockSpec((HB, nc, CHUNK, CHUNK), lambda i, hg: (hg, i, 0, 0))]
    return _pcall(body, out_shape=out_shape, grid=(T // tt, nhg), in_specs=in_specs, out_specs=out_specs,
                  scratch_shapes=[pltpu.VMEM((H, HEAD_DIM, HEAD_DIM), f32)],
                  compiler_params=_cparams(("arbitrary", "arbitrary")), name=name)(qkv, qkv, qkv, z, z, alog_vec, dtb_vec, nw)


def gdn_bwd(do, qkv, z, zg_col0, zs_col0, alog_vec, dtb_vec, nw, ssave, isave, name):
    T, D3 = qkv.shape
    D = D3 // 3
    H, HB, tt, nc = _gdn_dims(T, D)
    W = HB * HEAD_DIM
    nhg = H // HB
    nt = T // tt
    zgb, zsb = zg_col0 // W, zs_col0 // LANES

    def body(do_ref, q_ref, k_ref, v_ref, zg_ref, zs_ref, alog_ref, dtb_ref, nw_ref, ssave_ref, isave_ref,
             dq_ref, dk_ref, dv_ref, dzg_ref, dzs_ref, dalog_ref, ddtb_ref, dnw_ref, ds_ref):
        i, hg = pl.program_id(0), pl.program_id(1)
        hsl = pl.ds(hg * HB, HB)

        @pl.when(i == 0)
        def _():
            ds_ref[hsl] = jnp.zeros((HB, HEAD_DIM, HEAD_DIM), f32)

        @pl.when(jnp.logical_and(i == 0, hg == 0))
        def _():
            dalog_ref[...] = jnp.zeros_like(dalog_ref)
            ddtb_ref[...] = jnp.zeros_like(ddtb_ref)
            dnw_ref[...] = jnp.zeros_like(dnw_ref)

        @pl.when(hg == 0)
        def _():
            dzs_ref[...] = jnp.zeros_like(dzs_ref)

        alog, dtb, nwv = alog_ref[...], dtb_ref[...], nw_ref[...]

        def step(cc, carry):
            c = nc - 1 - cc
            rows = pl.ds(pl.multiple_of(c * CHUNK, CHUNK), CHUNK)
            S = ssave_ref[:, pl.ds(c, 1)][:, 0]
            inv = isave_ref[:, pl.ds(c, 1)][:, 0]

            def fn(*xs):
                S_new, o, _ = _gdn_chunk(H, hg * HB, inv, *xs)
                return S_new, o

            _, pull = jax.vjp(fn, S, _split_heads(q_ref[rows, :], HB), _split_heads(k_ref[rows, :], HB),
                              _split_heads(v_ref[rows, :], HB), _split_heads(zg_ref[rows, :], HB),
                              zs_ref[rows, :], alog, dtb, nwv)
            dS, dq, dk, dv, dzg, dzs, dal, ddt, dnw = pull((ds_ref[hsl], _split_heads(do_ref[rows, :], HB)))
            ds_ref[hsl] = dS
            dq_ref[rows, :] = _merge_heads(dq)
            dk_ref[rows, :] = _merge_heads(dk)
            dv_ref[rows, :] = _merge_heads(dv)
            dzg_ref[rows, :] = _merge_heads(dzg).astype(bf16)
            dzs_ref[rows, :] += dzs
            dalog_ref[...] += dal
            ddtb_ref[...] += ddt
            dnw_ref[...] += dnw
            return carry

        lax.fori_loop(0, nc, step, 0)

    rev = lambda i: nt - 1 - i
    col = lambda off: pl.BlockSpec((tt, W), lambda i, hg, off=off: (rev(i), off + hg))
    vec = lambda r: pl.BlockSpec((r, LANES), lambda i, hg: (0, 0))
    in_specs = [col(0), col(0), col(nhg), col(2 * nhg), col(zgb), pl.BlockSpec((tt, LANES), lambda i, hg: (rev(i), zsb)),
                vec(1), vec(1), vec(1),
                pl.BlockSpec((HB, nc, HEAD_DIM, HEAD_DIM), lambda i, hg: (hg, rev(i), 0, 0)),
                pl.BlockSpec((HB, nc, CHUNK, CHUNK), lambda i, hg: (hg, rev(i), 0, 0))]
    out_shape = [SDS((T, D), f32), SDS((T, D), f32), SDS((T, D), f32), SDS((T, D), bf16), SDS((T, LANES), f32),
                 SDS((1, LANES), f32), SDS((1, LANES), f32), SDS((1, LANES), f32)]
    out_specs = [col(0), col(0), col(0), col(0), pl.BlockSpec((tt, LANES), lambda i, hg: (rev(i), 0)),
                 vec(1), vec(1), vec(1)]
    return _pcall(body, out_shape=out_shape, grid=(nt, nhg), in_specs=in_specs, out_specs=out_specs,
                  scratch_shapes=[pltpu.VMEM((H, HEAD_DIM, HEAD_DIM), f32)],
                  compiler_params=_cparams(("arbitrary", "arbitrary")), name=name)(
        do, qkv, qkv, qkv, z, z, alog_vec, dtb_vec, nw, ssave, isave)


def colsum(a, name, tt=512):
    T, N = a.shape
    tt = min(tt, T)
    tn = _pick(N, 2048, LANES)

    def body(a_ref, o_ref):
        i = pl.program_id(1)
        s = jnp.sum(a_ref[...].astype(f32), axis=0, keepdims=True)

        @pl.when(i == 0)
        def _():
            o_ref[...] = s

        @pl.when(i > 0)
        def _():
            o_ref[...] += s

    return _pcall(body, out_shape=SDS((1, N), f32), grid=(N // tn, T // tt),
                  in_specs=[pl.BlockSpec((tt, tn), lambda j, i: (i, j))], out_specs=pl.BlockSpec((1, tn), lambda j, i: (0, j)),
                  compiler_params=_cparams(("parallel", "arbitrary")), name=name)(a)


def loss_and_grad(y, target, name, tt=256):
    T, D = y.shape
    tt = min(tt, T)

    def body(y_ref, t_ref, loss_ref, dy_ref):
        i = pl.program_id(0)
        e = y_ref[...] - t_ref[...]
        dy_ref[...] = e * (1.0 / D)
        part = jnp.sum(jnp.sum(e * e, axis=1, keepdims=True), axis=0, keepdims=True) * (0.5 / D)
        part = jnp.broadcast_to(part, (1, LANES))

        @pl.when(i == 0)
        def _():
            loss_ref[...] = part

        @pl.when(i > 0)
        def _():
            loss_ref[...] += part

    blk = pl.BlockSpec((tt, D), lambda i: (i, 0))
    return _pcall(body, out_shape=[SDS((1, LANES), f32), SDS((T, D), f32)], grid=(T // tt,), in_specs=[blk, blk],
                  out_specs=[pl.BlockSpec((1, LANES), lambda i: (0, 0)), blk],
                  compiler_params=_cparams(("arbitrary",)), name=name)(y, target)


def adamw(w, g, m, v, name):
    L, R, C = w.shape
    tr = R if R % 8 else _pick(R, max(8, (1 << 19) // C // 8 * 8), 8)
    c1 = 1.0 / (1.0 - ADAM_B1 ** ADAM_STEP)
    c2 = 1.0 / (1.0 - ADAM_B2 ** ADAM_STEP)

    def body(w_ref, g_ref, m_ref, v_ref, d_ref, nm_ref, nv_ref):
        gg = g_ref[...]
        nm = ADAM_B1 * m_ref[...] + (1.0 - ADAM_B1) * gg
        nv = ADAM_B2 * v_ref[...] + (1.0 - ADAM_B2) * (gg * gg)
        m_hat = nm * c1
        v_hat = nv * c2
        d_ref[...] = -ADAM_LR * (m_hat / (jnp.sqrt(v_hat) + ADAM_EPS) + ADAM_WD * w_ref[...])
        nm_ref[...] = nm
        nv_ref[...] = nv

    blk = pl.BlockSpec((1, tr, C), lambda l, r: (l, r, 0))
    shp = SDS((L, R, C), f32)
    return _pcall(body, out_shape=[shp, shp, shp], grid=(L, R // tr), in_specs=[blk] * 4, out_specs=[blk] * 3,
                  compiler_params=_cparams(("parallel", "parallel")), name=name)(w, g, m, v)


HBM_SPEC = pl.BlockSpec(memory_space=pltpu.HBM)


def _dma_sems(n):
    return pltpu.SemaphoreType.DMA((n,))


def all_gather(shards, name):
    n = len(shards)

    def body(*refs):
        x_refs, out_refs = refs[:n], refs[n:2 * n]
        send_sems, recv_sems, local_sems = refs[2 * n:]
        x, y, c = lax.axis_index("x"), lax.axis_index("y"), lax.axis_index("c")
        me, sibling = (x, y, c), (x, y, 1 - c)
        chips = [(1 - x, y), (x, 1 - y), (1 - x, 1 - y)]

        def slot(t, px, py, pc):
            return out_refs[t].at[4 * px + 2 * py + pc]

        def copy(t, k, block, to, src=None):
            return pltpu.make_async_remote_copy(src_ref=slot(t, *block) if src is None else src, dst_ref=slot(t, *block),
                                                send_sem=send_sems.at[7 * t + k], recv_sem=recv_sems.at[7 * t + k],
                                                device_id=to, device_id_type=MESH)

        mine = [pltpu.make_async_copy(x_refs[t], slot(t, *me), local_sems.at[t]) for t in range(n)]
        for cp in mine:
            cp.start()
        first = []
        for t in range(n):
            first.append(copy(t, 0, me, sibling, src=x_refs[t]))
            first += [copy(t, 1 + j, me, (*chip, c), src=x_refs[t]) for j, chip in enumerate(chips)]
        for cp in first:
            cp.start()
        passed = []
        for j, chip in enumerate(chips):
            for t in range(n):
                copy(t, 1 + j, (*chip, c), me).wait_recv()
                cp = copy(t, 4 + j, (*chip, c), sibling)
                cp.start()
                passed.append(cp)
        for t in range(n):
            copy(t, 0, sibling, me).wait_recv()
        for j, chip in enumerate(chips):
            for t in range(n):
                copy(t, 4 + j, (*chip, 1 - c), me).wait_recv()
        for cp in first + passed:
            cp.wait_send()
        for cp in mine:
            cp.wait()

    return _pcall(body, out_shape=[SDS((N_DEV,) + s.shape, s.dtype) for s in shards], in_specs=[HBM_SPEC] * n,
                  out_specs=[HBM_SPEC] * n, scratch_shapes=[_dma_sems(7 * n), _dma_sems(7 * n), _dma_sems(n)],
                  name=name)(*shards)


def exchange_sibling(parts, name):
    n = len(parts)

    def body(*refs):
        p_refs, r_refs = refs[:n], refs[n:2 * n]
        send_sems, recv_sems = refs[2 * n:]
        x, y, c = lax.axis_index("x"), lax.axis_index("y"), lax.axis_index("c")
        cps = []
        for t in range(n):
            for k in range(4):
                cps.append(pltpu.make_async_remote_copy(src_ref=p_refs[t].at[2 * k + 1 - c], dst_ref=r_refs[t].at[k],
                                                        send_sem=send_sems.at[4 * t + k], recv_sem=recv_sems.at[4 * t + k],
                                                        device_id=(x, y, 1 - c), device_id_type=MESH))
        for cp in cps:
            cp.start()
        for cp in cps:
            cp.wait_recv()
        for cp in cps:
            cp.wait_send()

    return _pcall(body, out_shape=[SDS((4,) + p.shape[1:], p.dtype) for p in parts], in_specs=[HBM_SPEC] * n,
                  out_specs=[HBM_SPEC] * n, scratch_shapes=[_dma_sems(4 * n), _dma_sems(4 * n)], name=name)(*parts)


def exchange_chips(s1s, name):
    n = len(s1s)

    def body(*refs):
        s_refs, r_refs = refs[:n], refs[n:2 * n]
        send_sems, recv_sems, local_sems = refs[2 * n:]
        x, y, c = lax.axis_index("x"), lax.axis_index("y"), lax.axis_index("c")
        my_chip = 2 * x + y
        chips = [(1 - x, y), (x, 1 - y), (1 - x, 1 - y)]
        mine = [pltpu.make_async_copy(s_refs[t].at[my_chip], r_refs[t].at[my_chip], local_sems.at[t]) for t in range(n)]
        for cp in mine:
            cp.start()
        cps = []
        for t in range(n):
            for j, (px, py) in enumerate(chips):
                cps.append(pltpu.make_async_remote_copy(src_ref=s_refs[t].at[2 * px + py], dst_ref=r_refs[t].at[my_chip],
                                                        send_sem=send_sems.at[3 * t + j], recv_sem=recv_sems.at[3 * t + j],
                                                        device_id=(px, py, c), device_id_type=MESH))
        for cp in cps:
            cp.start()
        for t in range(n):
            for j, (px, py) in enumerate(chips):
                pltpu.make_async_remote_copy(src_ref=s_refs[t].at[my_chip], dst_ref=r_refs[t].at[2 * px + py],
                                             send_sem=send_sems.at[3 * t + j], recv_sem=recv_sems.at[3 * t + j],
                                             device_id=(px, py, c), device_id_type=MESH).wait_recv()
        for cp in cps:
            cp.wait_send()
        for cp in mine:
            cp.wait()

    return _pcall(body, out_shape=[SDS(s.shape, s.dtype) for s in s1s], in_specs=[HBM_SPEC] * n, out_specs=[HBM_SPEC] * n,
                  scratch_shapes=[_dma_sems(3 * n), _dma_sems(3 * n), _dma_sems(n)], name=name)(*s1s)


def _handshake(peers):
    barrier = pltpu.get_barrier_semaphore()
    for peer in peers:
        pl.semaphore_signal(barrier, inc=1, device_id=peer, device_id_type=MESH)
    pl.semaphore_wait(barrier, len(peers))


def _hbm_ref(a):
    return jax.new_ref(a, memory_space=pltpu.MemorySpace.HBM)


def _hbm_empty(shape, dtype):
    return jax.empty_ref(SDS(shape, dtype), memory_space=pltpu.MemorySpace.HBM)


def all_gather_behind(shards, name, collective_id):
    n = len(shards)
    x_refs = [_hbm_ref(s) for s in shards]
    out_refs = [_hbm_empty((N_DEV,) + s.shape, s.dtype) for s in shards]

    @pl.kernel(mesh=plsc.ScalarSubcoreMesh(axis_name="sequencer", num_cores=1), name=name,
               scratch_types=(_dma_sems(7 * n), _dma_sems(7 * n), _dma_sems(n)),
               compiler_params=pltpu.CompilerParams(collective_id=collective_id))
    def launch(send_sems, recv_sems, local_sems):
        x, y, c = lax.axis_index("x"), lax.axis_index("y"), lax.axis_index("c")
        me, sibling = (x, y, c), (x, y, 1 - c)
        chips = [(1 - x, y), (x, 1 - y), (1 - x, 1 - y)]
        _handshake([sibling] + [(*chip, c) for chip in chips])

        def slot(t, px, py, pc):
            return out_refs[t].at[4 * px + 2 * py + pc]

        def copy(t, k, block, to, src=None):
            return pltpu.make_async_remote_copy(src_ref=slot(t, *block) if src is None else src, dst_ref=slot(t, *block),
                                                send_sem=send_sems.at[7 * t + k], recv_sem=recv_sems.at[7 * t + k],
                                                device_id=to, device_id_type=MESH)

        mine = [pltpu.make_async_copy(x_refs[t], slot(t, *me), local_sems.at[t]) for t in range(n)]
        for cp in mine:
            cp.start()
        first = []
        for t in range(n):
            first.append(copy(t, 0, me, sibling, src=x_refs[t]))
            first += [copy(t, 1 + j, me, (*chip, c), src=x_refs[t]) for j, chip in enumerate(chips)]
        for cp in first:
            cp.start()
        passed = []
        for j, chip in enumerate(chips):
            for t in range(n):
                copy(t, 1 + j, (*chip, c), me).wait_recv()
                cp = copy(t, 4 + j, (*chip, c), sibling)
                cp.start()
                passed.append(cp)
        for t in range(n):
            copy(t, 0, sibling, me).wait_recv()
        for j, chip in enumerate(chips):
            for t in range(n):
                copy(t, 4 + j, (*chip, 1 - c), me).wait_recv()
        for cp in first + passed:
            cp.wait_send()
        for cp in mine:
            cp.wait()

    launch()
    return out_refs


def exchange_chips_behind(s1s, name, collective_id):
    n = len(s1s)
    s_refs = [_hbm_ref(s) for s in s1s]
    r_refs = [_hbm_empty(s.shape, s.dtype) for s in s1s]

    @pl.kernel(mesh=plsc.ScalarSubcoreMesh(axis_name="sequencer", num_cores=1), name=name,
               scratch_types=(_dma_sems(3 * n), _dma_sems(3 * n), _dma_sems(n)),
               compiler_params=pltpu.CompilerParams(collective_id=collective_id))
    def launch(send_sems, recv_sems, local_sems):
        x, y, c = lax.axis_index("x"), lax.axis_index("y"), lax.axis_index("c")
        my_chip = 2 * x + y
        chips = [(1 - x, y), (x, 1 - y), (1 - x, 1 - y)]
        _handshake([(*chip, c) for chip in chips])
        mine = [pltpu.make_async_copy(s_refs[t].at[my_chip], r_refs[t].at[my_chip], local_sems.at[t]) for t in range(n)]
        for cp in mine:
            cp.start()
        cps = []
        for t in range(n):
            for j, (px, py) in enumerate(chips):
                cps.append(pltpu.make_async_remote_copy(src_ref=s_refs[t].at[2 * px + py], dst_ref=r_refs[t].at[my_chip],
                                                        send_sem=send_sems.at[3 * t + j], recv_sem=recv_sems.at[3 * t + j],
                                                        device_id=(px, py, c), device_id_type=MESH))
        for cp in cps:
            cp.start()
        for t in range(n):
            for j, (px, py) in enumerate(chips):
                pltpu.make_async_remote_copy(src_ref=s_refs[t].at[my_chip], dst_ref=r_refs[t].at[2 * px + py],
                                             send_sem=send_sems.at[3 * t + j], recv_sem=recv_sems.at[3 * t + j],
                                             device_id=(px, py, c), device_id_type=MESH).wait_recv()
        for cp in cps:
            cp.wait_send()
        for cp in mine:
            cp.wait()

    launch()
    return r_refs


def _rows_tile(r, c, itemsize):
    return _pick(r, max(16, (1 << 20) // (c * itemsize) // 16 * 16), 16)


def pair_add(part, got, name):
    _, r, c = part.shape
    tr = _rows_tile(r, c, 2)
    core = lax.axis_index("c").reshape(1).astype(jnp.int32)

    def body(core_ref, p_ref, g_ref, o_ref):
        o_ref[...] = (p_ref[...].astype(f32) + g_ref[...].astype(f32)).astype(bf16)

    blk = pl.BlockSpec((None, tr, c), lambda k, i, core_ref: (k, i, 0))
    gs = pltpu.PrefetchScalarGridSpec(
        num_scalar_prefetch=1, grid=(4, r // tr),
        in_specs=[pl.BlockSpec((None, None, tr, c), lambda k, i, core_ref: (k, core_ref[0], i, 0)), blk], out_specs=blk)
    return _pcall(body, out_shape=SDS((4, r, c), bf16), grid_spec=gs, compiler_params=_cparams(("parallel", "parallel")),
                  name=name)(core, part.reshape(4, 2, r, c), got)


def slot_sum(a, name):
    S, r, c = a.shape
    tr = _rows_tile(r, c, a.dtype.itemsize * S)

    def body(a_ref, o_ref):
        acc = a_ref[0].astype(f32)
        for s in range(1, S):
            acc = acc + a_ref[s].astype(f32)
        o_ref[...] = acc

    return _pcall(body, out_shape=SDS((r, c), f32), grid=(r // tr,), in_specs=[pl.BlockSpec((S, tr, c), lambda i: (0, i, 0))],
                  out_specs=pl.BlockSpec((tr, c), lambda i: (i, 0)), compiler_params=_cparams(("parallel",)), name=name)(a)


def reduce_scatter_start(parts, name, collective_id):
    got = exchange_sibling(parts, name + "_c")
    s1 = [pair_add(p, g, f"{name}_add{t}") for t, (p, g) in enumerate(zip(parts, got))]
    return exchange_chips_behind(s1, name + "_xy", collective_id)


def reduce_scatter_finish(refs, name):
    return [slot_sum(r[...], f"{name}_sum{t}") for t, r in enumerate(refs)]


def _to_pack(flat, dtype):
    n = flat.shape[-1]
    unit = 16 * PACK_COLS
    padded = -(-n // unit) * unit
    return jnp.pad(flat.astype(dtype), (0, padded - n)).reshape(padded // PACK_COLS, PACK_COLS)


BIG = ("w_in", "w_conv_proj", "w_gdn_proj", "w_out", "w_ffn_in", "w_ffn_out")
COL_SHARDED = ("w_in", "w_ffn_in", "conv_dw_w", "short_conv_w")
SMALL = ("b_in", "conv_dw_b", "conv_ln_g", "conv_ln_b", "b_conv_proj", "a_log", "dt_bias", "gdn_norm_w",
         "ln1_g", "ln1_b", "ln2_g", "ln2_b")
CONVW = ("conv_dw_w", "short_conv_w")
ORDER = ("w_in", "b_in", "conv_dw_w", "conv_dw_b", "conv_ln_g", "conv_ln_b", "w_conv_proj", "b_conv_proj",
         "short_conv_w", "a_log", "dt_bias", "gdn_norm_w", "w_gdn_proj", "w_out", "ln1_g", "ln1_b",
         "w_ffn_in", "w_ffn_out", "ln2_g", "ln2_b")


def _full_from_gathered(g, name):
    if name in COL_SHARDED:
        return jnp.moveaxis(g, 0, 1).reshape(g.shape[1], N_DEV * g.shape[2])
    return g.reshape(N_DEV * g.shape[1], g.shape[2])


def _w_in_perm(w, D, H):
    pad = jnp.zeros(w.shape[:-1] + (2 * LANES - 2 * H,), w.dtype)
    return jnp.concatenate([w[..., :6 * D], w[..., 6 * D + 2 * H:], w[..., 6 * D:6 * D + 2 * H], pad], axis=-1)


def _w_in_unperm(w, D, H):
    return jnp.concatenate([w[..., :6 * D], w[..., 8 * D:8 * D + 2 * H], w[..., 6 * D:8 * D]], axis=-1)


def kernel(x, w_in, b_in, conv_dw_w, conv_dw_b, conv_ln_g, conv_ln_b, w_conv_proj, b_conv_proj, short_conv_w, a_log, dt_bias, gdn_norm_w, w_gdn_proj, w_out, ln1_g, ln1_b, w_ffn_in, w_ffn_out, ln2_g, ln2_b, loss_target, m_w_in, m_b_in, m_conv_dw_w, m_conv_dw_b, m_conv_ln_g, m_conv_ln_b, m_w_conv_proj, m_b_conv_proj, m_short_conv_w, m_a_log, m_dt_bias, m_gdn_norm_w, m_w_gdn_proj, m_w_out, m_ln1_g, m_ln1_b, m_w_ffn_in, m_w_ffn_out, m_ln2_g, m_ln2_b, v_w_in, v_b_in, v_conv_dw_w, v_conv_dw_b, v_conv_ln_g, v_conv_ln_b, v_w_conv_proj, v_b_conv_proj, v_short_conv_w, v_a_log, v_dt_bias, v_gdn_norm_w, v_w_gdn_proj, v_w_out, v_ln1_g, v_ln1_b, v_w_ffn_in, v_w_ffn_out, v_ln2_g, v_ln2_b):
    W = dict(w_in=w_in, b_in=b_in, conv_dw_w=conv_dw_w, conv_dw_b=conv_dw_b, conv_ln_g=conv_ln_g, conv_ln_b=conv_ln_b,
             w_conv_proj=w_conv_proj, b_conv_proj=b_conv_proj, short_conv_w=short_conv_w, a_log=a_log, dt_bias=dt_bias,
             gdn_norm_w=gdn_norm_w, w_gdn_proj=w_gdn_proj, w_out=w_out, ln1_g=ln1_g, ln1_b=ln1_b, w_ffn_in=w_ffn_in,
             w_ffn_out=w_ffn_out, ln2_g=ln2_g, ln2_b=ln2_b)
    MO = dict(w_in=m_w_in, b_in=m_b_in, conv_dw_w=m_conv_dw_w, conv_dw_b=m_conv_dw_b, conv_ln_g=m_conv_ln_g,
              conv_ln_b=m_conv_ln_b, w_conv_proj=m_w_conv_proj, b_conv_proj=m_b_conv_proj, short_conv_w=m_short_conv_w,
              a_log=m_a_log, dt_bias=m_dt_bias, gdn_norm_w=m_gdn_norm_w, w_gdn_proj=m_w_gdn_proj, w_out=m_w_out,
              ln1_g=m_ln1_g, ln1_b=m_ln1_b, w_ffn_in=m_w_ffn_in, w_ffn_out=m_w_ffn_out, ln2_g=m_ln2_g, ln2_b=m_ln2_b)
    VO = dict(w_in=v_w_in, b_in=v_b_in, conv_dw_w=v_conv_dw_w, conv_dw_b=v_conv_dw_b, conv_ln_g=v_conv_ln_g,
              conv_ln_b=v_conv_ln_b, w_conv_proj=v_w_conv_proj, b_conv_proj=v_b_conv_proj, short_conv_w=v_short_conv_w,
              a_log=v_a_log, dt_bias=v_dt_bias, gdn_norm_w=v_gdn_norm_w, w_gdn_proj=v_w_gdn_proj, w_out=v_w_out,
              ln1_g=v_ln1_g, ln1_b=v_ln1_b, w_ffn_in=v_w_ffn_in, w_ffn_out=v_w_ffn_out, ln2_g=v_ln2_g, ln2_b=v_ln2_b)

    _, T, D = x.shape
    DEPTH = w_in.shape[0]
    H = D // HEAD_DIM
    F = w_ffn_out.shape[1] * N_DEV
    alpha = (2.0 * DEPTH) ** 0.25
    lnres_fn = make_lnres_fn(alpha)
    TT = min(256, T)
    dev = 4 * lax.axis_index("x") + 2 * lax.axis_index("y") + lax.axis_index("c")

    cw_flat = jnp.concatenate([W[n].reshape(-1) for n in CONVW])
    cw_g = all_gather([_to_pack(cw_flat, f32)], "ag_convw")[0].reshape(N_DEV, -1)
    conv_full, off = {}, 0
    for n in CONVW:
        sz = W[n].size
        blk = cw_g[:, off:off + sz].reshape((N_DEV,) + W[n].shape)
        conv_full[n] = jnp.moveaxis(blk, 0, 2).reshape(W[n].shape[0], W[n].shape[1], -1)
        off += sz

    def gather_layer_start(l):
        return all_gather_behind([W[n][l].astype(bf16) for n in BIG], f"ag_w{l}", l)

    def gather_layer_finish(refs, after):
        vals, _ = lax.optimization_barrier(([r[...] for r in refs], after))
        g = dict(zip(BIG, vals))
        out = {n: g[n].reshape(-1, g[n].shape[2]) for n in BIG if n not in COL_SHARDED}
        out["w_ffn_in"] = g["w_ffn_in"]
        out["w_in"] = _w_in_perm(_full_from_gathered(g["w_in"], "w_in"), D, H)
        return out

    def lane_vec(v, off=0):
        return jnp.pad(v, (off, LANES - off - v.shape[0])).reshape(1, LANES)

    def row(v):
        return v.reshape(1, -1)

    h32 = x.reshape(T, D)
    h16 = h32.astype(bf16)
    saved = []
    gathering = gather_layer_start(0)
    for l in range(DEPTH):
        G = gather_layer_finish(gathering, h32)
        if l + 1 < DEPTH:
            gathering = gather_layer_start(l + 1)
        b_all = _w_in_perm(b_in[l], D, H)
        alog_vec, dtb_vec = lane_vec(a_log[l], H), lane_vec(dt_bias[l], H)
        nw = row(gdn_norm_w[l])
        z = matmul(h16, G["w_in"], "nn", f"l{l}_mm_in", bias=b_all)
        c0, = rowwise(glu_fn, [(z, 0, D), (z, 1, D)], [], [(f32,)], TT, f"l{l}_glu")
        c1 = conv_fwd(c0, 0, conv_full["conv_dw_w"][l], 0, D, CONV_WIDTH, row(conv_dw_b[l]), f"l{l}_conv")
        c3, = rowwise(lnsilu_fn, [(c1, 0, D)], [row(conv_ln_g[l]), row(conv_ln_b[l])], [(bf16,)], TT, f"l{l}_lnsilu")
        yc = matmul(c3, G["w_conv_proj"], "nn", f"l{l}_mm_cp", bias=b_conv_proj[l])
        qkv = conv_fwd(z, 2 * D, conv_full["short_conv_w"][l], 0, 3 * D, SHORT_CONV, None, f"l{l}_sconv", tt=1024)
        og, ssave, isave = gdn_fwd(qkv, z, 5 * D, 8 * D, alog_vec, dtb_vec, nw, f"l{l}_gdn")
        yg = matmul(og, G["w_gdn_proj"], "nn", f"l{l}_mm_gp")
        m, = rowwise(merge_fn, [(z, 6, D), (z, 7, D), (yc, 0, D), (yg, 0, D)], [], [(bf16,)], TT, f"l{l}_merge")
        mix = matmul(m, G["w_out"], "nn", f"l{l}_mm_out")
        x1_32, x1_16 = rowwise(lnres_fn, [(h32, 0, D), (mix, 0, D)], [row(ln1_g[l]), row(ln1_b[l])], [(f32, bf16)], TT, f"l{l}_ln1")
        hf = matmul(x1_16, G["w_ffn_in"], "nn", f"l{l}_mm_fi", b_blocked=True)
        act, = rowwise(swiglu_fn, [(hf, 0, F), (hf, 1, F)], [], [(bf16,)], min(128, T), f"l{l}_swiglu")
        ff = matmul(act, G["w_ffn_out"], "nn", f"l{l}_mm_fo")
        x2_32, x2_16 = rowwise(lnres_fn, [(x1_32, 0, D), (ff, 0, D)], [row(ln2_g[l]), row(ln2_b[l])], [(f32, bf16)], TT, f"l{l}_ln2")
        saved.append(dict(G=G, h32=h32, h16=h16, z=z, c0=c0, c1=c1, c3=c3, yc=yc, qkv=qkv, ssave=ssave, isave=isave, og=og, yg=yg, m=m,
                          mix=mix, x1_32=x1_32, x1_16=x1_16, hf=hf, act=act, ff=ff, alog_vec=alog_vec, dtb_vec=dtb_vec, nw=nw))
        h32, h16 = x2_32, x2_16

    loss_vec, dy = loss_and_grad(h32, loss_target.reshape(T, D), "loss")
    loss = lax.psum(loss_vec[0, 0], AXES)

    gsmall = {n: [None] * DEPTH for n in SMALL + CONVW}
    gbig = {n: [None] * DEPTH for n in BIG}
    dh = [dy]
    reducing = []

    def finish_reduce(l, refs):
        for n, red in zip(BIG, reduce_scatter_finish(refs, f"rs{l}")):
            gbig[n][l] = red

    for l in reversed(range(DEPTH)):
        s = saved[l]
        G = s["G"]
        TB = min(128, T)
        d_x1r, d_ff, dg2, db2 = rowwise_vjp(lnres_fn, [(s["x1_32"], 0, D), (s["ff"], 0, D)], [row(ln2_g[l]), row(ln2_b[l])],
                                            [dh], [f32, bf16], TB, f"l{l}_ln2_b")
        d_act = matmul(d_ff, G["w_ffn_out"], "nt", f"l{l}_mm_fo_dx", tn_cap=1408)
        dw_fo = matmul(s["act"], d_ff, "tn", f"l{l}_mm_fo_dw", tm_cap=1408, tn_cap=2048, tk_cap=1024, out_dtype=bf16)
        d_hf, = rowwise_vjp(swiglu_fn, [(s["hf"], 0, F), (s["hf"], 1, F)], [], [[d_act]], [bf16, bf16], min(64, T), f"l{l}_swiglu_b",
                            pack=True)
        d_x1m = matmul(d_hf, G["w_ffn_in"], "nt", f"l{l}_mm_fi_dx", b_blocked=True)
        dw_fi = matmul(s["x1_16"], d_hf, "tn", f"l{l}_mm_fi_dw", tk_cap=1024, out_dtype=bf16, out_blocked=True)
        d_hr, d_mix, dg1, db1 = rowwise_vjp(lnres_fn, [(s["h32"], 0, D), (s["mix"], 0, D)], [row(ln1_g[l]), row(ln1_b[l])],
                                            [[d_x1r, d_x1m]], [f32, bf16], TB, f"l{l}_ln1_b")
        d_m = matmul(d_mix, G["w_out"], "nt", f"l{l}_mm_out_dx")
        dw_out = matmul(s["m"], d_mix, "tn", f"l{l}_mm_out_dw", tk_cap=1024, out_dtype=bf16)
        d_ga, d_gb, d_yc, d_yg = rowwise_vjp(merge_fn, [(s["z"], 6, D), (s["z"], 7, D), (s["yc"], 0, D), (s["yg"], 0, D)], [],
                                             [[d_m]], [bf16] * 4, TB, f"l{l}_merge_b")
        d_c3 = matmul(d_yc, G["w_conv_proj"], "nt", f"l{l}_mm_cp_dx")
        dw_cp = matmul(s["c3"], d_yc, "tn", f"l{l}_mm_cp_dw", tk_cap=1024, out_dtype=bf16)
        db_cp = colsum(d_yc, f"l{l}_cs_cp")
        d_c1, dcg, dcb = rowwise_vjp(lnsilu_fn, [(s["c1"], 0, D)], [row(conv_ln_g[l]), row(conv_ln_b[l])], [[d_c3]], [f32], TB, f"l{l}_lnsilu_b")
        d_c0, dw31, db31 = conv_bwd(d_c1, s["c0"], 0, conv_full["conv_dw_w"][l], 0, D, CONV_WIDTH, f32, f"l{l}_conv_b")
        d_glu_a, d_glu_b = rowwise_vjp(glu_fn, [(s["z"], 0, D), (s["z"], 1, D)], [], [[d_c0]], [bf16, bf16], TB, f"l{l}_glu_b")
        d_og = matmul(d_yg, G["w_gdn_proj"], "nt", f"l{l}_mm_gp_dx")
        dw_gp = matmul(s["og"], d_yg, "tn", f"l{l}_mm_gp_dw", tk_cap=1024, out_dtype=bf16)
        dq, dk, dv, d_zg, d_zs, dalog, ddtb, dnw = gdn_bwd(d_og, s["qkv"], s["z"], 5 * D, 8 * D, s["alog_vec"], s["dtb_vec"], s["nw"],
                                                            s["ssave"], s["isave"], f"l{l}_gdn_b")
        dxs, dwss = [], []
        for sec, dsec in enumerate((dq, dk, dv)):
            dxp, dwp, _ = conv_bwd(dsec, s["z"], (2 + sec) * D, conv_full["short_conv_w"][l], sec * D, D, SHORT_CONV, bf16,
                                   f"l{l}_sconv_b{sec}", tt=1024)
            dxs.append(dxp)
            dwss.append(dwp)
        dz = jnp.concatenate([d_glu_a, d_glu_b] + dxs + [d_zg, d_ga, d_gb, d_zs.astype(bf16), jnp.zeros((T, LANES), bf16)], axis=1)
        d_hm, db_all = matmul(dz, G["w_in"], "nt", f"l{l}_mm_in_dx", tk_cap=3328, colsum_a=True)
        dw_in = matmul(s["h16"], dz, "tn", f"l{l}_mm_in_dw", tk_cap=1024, out_dtype=bf16)
        dh = [d_hr, d_hm]

        gsmall["b_in"][l] = _w_in_unperm(db_all[0], D, H)
        gsmall["conv_dw_b"][l] = db31[0]
        gsmall["conv_ln_g"][l], gsmall["conv_ln_b"][l] = dcg[0], dcb[0]
        gsmall["b_conv_proj"][l] = db_cp[0]
        gsmall["a_log"][l], gsmall["dt_bias"][l] = dalog[0, H:2 * H], ddtb[0, H:2 * H]
        gsmall["gdn_norm_w"][l] = dnw[0]
        gsmall["ln1_g"][l], gsmall["ln1_b"][l], gsmall["ln2_g"][l], gsmall["ln2_b"][l] = dg1[0], db1[0], dg2[0], db2[0]
        gsmall["conv_dw_w"][l] = dw31
        gsmall["short_conv_w"][l] = jnp.concatenate(dwss, axis=1)

        dw_in_u = _w_in_unperm(dw_in, D, H)
        blocks = dict(w_in=jnp.moveaxis(dw_in_u.reshape(D, N_DEV, -1), 1, 0), w_ffn_in=dw_fi)
        for n, dw in (("w_conv_proj", dw_cp), ("w_gdn_proj", dw_gp), ("w_out", dw_out), ("w_ffn_out", dw_fo)):
            blocks[n] = dw.reshape(N_DEV, -1, dw.shape[1])
        reducing.append((l, reduce_scatter_start([blocks[n] for n in BIG], f"rs{l}", DEPTH + l)))
        if len(reducing) > 1:
            finish_reduce(*reducing.pop(0))
    finish_reduce(*reducing.pop(0))

    grad_x, = rowwise(add_fn, [(dh[0], 0, D), (dh[1], 0, D)], [], [(f32,)], TT, "grad_x")
    grad_x = grad_x.reshape(1, T, D)

    small_flat = jnp.concatenate([jnp.stack(gsmall[n]).reshape(-1) for n in SMALL + CONVW])
    sg = all_gather([_to_pack(small_flat, f32)], "ag_small")[0]
    small_tot = slot_sum(sg, "small_sum").reshape(-1)
    grads, off = {}, 0
    for n in SMALL:
        grads[n] = small_tot[off:off + W[n].size].reshape(W[n].shape)
        off += W[n].size
    for n in CONVW:
        L_, K_, c_ = W[n].shape
        fullg = small_tot[off:off + L_ * K_ * c_ * N_DEV].reshape(L_, K_, c_ * N_DEV)
        grads[n] = lax.dynamic_slice_in_dim(fullg, dev * c_, c_, axis=2)
        off += L_ * K_ * c_ * N_DEV
    for n in BIG:
        grads[n] = jnp.stack(gbig[n])

    delta, new_m, new_v = {}, {}, {}
    for n in BIG + CONVW:
        delta[n], new_m[n], new_v[n] = adamw(W[n], grads[n], MO[n], VO[n], f"adamw_{n}")
    pk = lambda d: _to_pack(jnp.concatenate([d[n].reshape(-1) for n in SMALL]), f32)[None]
    ds, ms, vs = adamw(pk(W), pk(grads), pk(MO), pk(VO), "adamw_small")
    off = 0
    for n in SMALL:
        sl = lambda a: a.reshape(-1)[off:off + W[n].size].reshape(W[n].shape)
        delta[n], new_m[n], new_v[n] = sl(ds), sl(ms), sl(vs)
        off += W[n].size

    return (loss, grad_x, *[grads[n] for n in ORDER], *[delta[n] for n in ORDER],
            *[new_m[n] for n in ORDER], *[new_v[n] for n in ORDER])
```
